```python
import math
import jax, jax.numpy as jnp
from jax import lax
import numpy as np

D_MODEL = 1024
BATCH = 8
SEQ = 8192
DEPTH = 1
DEC_BATCH = 8
DEC_SEQ = 4096
PAST_LEN = 128

D_RWKV = 512
HEAD_DIM = 64
N_HEADS = D_RWKV // HEAD_DIM
D_S5 = D_MODEL - D_RWKV
S5_GROUP = 16
N_S5_GROUPS = D_S5 // S5_GROUP
S5_STATE = 64
DECAY_LORA = 64
ICLR_LORA = 64
GATE_LORA = 128
N_DIR = 2
D_FF = int(math.ceil(8 * D_MODEL / 3 / 256)) * 256
RWKV_COLS = 3 * D_RWKV + N_DIR * DECAY_LORA + N_DIR * ICLR_LORA + GATE_LORA
D_IN_PROJ = RWKV_COLS + D_S5
RMS_EPS = 1e-6
LNX_EPS = 64e-5

kernel_name = 'hybrid_rwkv7_s5_adaln_encoder'


def rmsnorm(x, g):
    xf = x.astype(jnp.float32)
    y = xf * lax.rsqrt(jnp.mean(xf * xf, axis=-1, keepdims=True) + RMS_EPS)
    return (y * g.astype(jnp.float32)).astype(x.dtype)


def centred_shift(p):
    prev = jnp.pad(p[:, :-1], ((0, 0), (1, 0), (0, 0)))
    nxt = jnp.pad(p[:, 1:], ((0, 0), (0, 1), (0, 0)))
    return 0.5 * (prev + nxt)


def rwkv7_scan(r, w, k, v, kk, a, reverse):
    bsz, _, n_h, n_d = r.shape

    def step(S, inp):
        r_t, w_t, k_t, v_t, kk_t, a_t = inp
        sa = jnp.einsum('bhvk,bhk->bhv', S, -kk_t)
        S = (S * w_t[:, :, None, :] + sa[..., None] * (kk_t * a_t)[:, :, None, :]
             + v_t[..., None] * k_t[:, :, None, :])
        return S, jnp.einsum('bhvk,bhk->bhv', S, r_t)

    xs = tuple(jnp.moveaxis(t, 1, 0) for t in (r, w, k, v, kk, a))
    S0 = jnp.zeros((bsz, n_h, n_d, n_d), jnp.float32)
    _, y = lax.scan(step, S0, xs, reverse=reverse)
    return jnp.moveaxis(y, 0, 1)


def rwkv7_mixer(p, mu_shift, w0, w2, a0, a2, g2, k_k, k_a, r_k, lnx_g, lnx_b):
    bsz, t_len, _ = p.shape
    p = p + mu_shift * (centred_shift(p) - p)
    pf = p.astype(jnp.float32)
    heads = lambda t: t.reshape(bsz, t_len, N_HEADS, HEAD_DIM)
    o = 0
    r = pf[..., o:o + D_RWKV]; o += D_RWKV
    k = pf[..., o:o + D_RWKV]; o += D_RWKV
    v = pf[..., o:o + D_RWKV]; o += D_RWKV
    xw = pf[..., o:o + N_DIR * DECAY_LORA].reshape(bsz, t_len, N_DIR, DECAY_LORA); o += N_DIR * DECAY_LORA
    xa = pf[..., o:o + N_DIR * ICLR_LORA].reshape(bsz, t_len, N_DIR, ICLR_LORA); o += N_DIR * ICLR_LORA
    xg = pf[..., o:o + GATE_LORA]
    g = jax.nn.sigmoid(xg) @ g2.astype(jnp.float32)
    kk = heads(k * k_k.astype(jnp.float32))
    kk = kk * lax.rsqrt(jnp.maximum(jnp.sum(kk * kk, axis=-1, keepdims=True), 1e-24))
    rh, vh = heads(r), heads(v)
    r_k_f = r_k.astype(jnp.float32)
    k_a_f = k_a.astype(jnp.float32)
    ys, bonuses = [], []
    for d in range(N_DIR):
        wd = -jax.nn.softplus(-(w0[d].astype(jnp.float32) + jnp.tanh(xw[:, :, d]) @ w2[d].astype(jnp.float32))) - 0.5
        decay = jnp.exp(-jnp.exp(wd))
        ad = jax.nn.sigmoid(a0[d].astype(jnp.float32) + xa[:, :, d] @ a2[d].astype(jnp.float32))
        kd = heads(k * (1.0 + (ad - 1.0) * k_a_f))
        ys.append(rwkv7_scan(rh, heads(decay), kd, vh, kk, heads(ad), reverse=(d == 1)))
        bonuses.append(jnp.sum(rh * kd * r_k_f, axis=-1, keepdims=True) * vh)
    y = ys[0] + ys[1]
    mean = jnp.mean(y, axis=-1, keepdims=True)
    var = jnp.mean(jnp.square(y - mean), axis=-1, keepdims=True)
    y = ((y - mean) * lax.rsqrt(var + LNX_EPS) * lnx_g.astype(jnp.float32).reshape(N_HEADS, HEAD_DIM)
         + lnx_b.astype(jnp.float32).reshape(N_HEADS, HEAD_DIM))
    y = y + bonuses[0] + bonuses[1]
    return (y.reshape(bsz, t_len, D_RWKV) * g).astype(p.dtype)


def _diag_combine(e1, e2):
    a1, b1 = e1
    a2, b2 = e2
    return (a1 * a2, a2 * b1 + b2)


def s5_mixer(u, lam_re, lam_im, log_dt, b_re, b_im, c_re, c_im, d_skip, w_glu, b_glu, s5_out_g):
    f32 = jnp.float32
    bsz, t_len, _ = u.shape
    uf = u.astype(f32).reshape(bsz, t_len, N_S5_GROUPS, S5_GROUP)
    uc = uf.astype(jnp.complex64)
    y = d_skip.astype(f32) * uf
    for d in range(N_DIR):
        lam = lax.complex(lam_re[d].astype(f32), lam_im[d].astype(f32))
        dt = jnp.exp(log_dt[d].astype(f32))[:, None]
        a_bar = jnp.exp(lam * dt)
        b_bar = ((a_bar - 1.0) / lam)[..., None] * lax.complex(b_re[d].astype(f32), b_im[d].astype(f32))
        bu = jnp.einsum('gph,btgh->btgp', b_bar, uc)
        _, states = lax.associative_scan(
            _diag_combine, (jnp.broadcast_to(a_bar, bu.shape), bu), reverse=(d == 1), axis=1)
        c_mat = lax.complex(c_re[d].astype(f32), c_im[d].astype(f32))
        y = y + jnp.einsum('ghp,btgp->btgh', c_mat, states).real
    z = jax.nn.gelu(y)
    z = z * jax.nn.sigmoid(jnp.einsum('btgh,ghk->btgk', z, w_glu.astype(f32)) + b_glu.astype(f32))
    return rmsnorm(z.reshape(bsz, t_len, D_S5), s5_out_g).astype(u.dtype)


def encoder_layer(x, c, norm1_g, w_ada, b_ada, w_in, mu_shift, w0, w2, a0, a2, g2, k_k, k_a, r_k,
                  lnx_g, lnx_b, lam_re, lam_im, log_dt, b_re, b_im, c_re, c_im, d_skip, w_glu, b_glu,
                  s5_out_g, w_out, norm2_g, w_ff1, w_ff3, w_ff2):
    mod = (jax.nn.silu(c) @ w_ada + b_ada)[:, None, :]
    shift1, scale1, gate1, shift2, scale2, gate2 = jnp.split(mod, 6, axis=-1)
    h = rmsnorm(x, norm1_g) * (1.0 + scale1) + shift1
    proj = h @ w_in
    y_rwkv = rwkv7_mixer(proj[..., :RWKV_COLS], mu_shift, w0, w2, a0, a2, g2, k_k, k_a, r_k, lnx_g, lnx_b)
    y_s5 = s5_mixer(proj[..., RWKV_COLS:], lam_re, lam_im, log_dt, b_re, b_im, c_re, c_im,
                    d_skip, w_glu, b_glu, s5_out_g)
    x = x + gate1 * (jnp.concatenate([y_rwkv, y_s5], axis=-1) @ w_out)
    h = rmsnorm(x, norm2_g) * (1.0 + scale2) + shift2
    f = (jax.nn.silu(h @ w_ff1) * (h @ w_ff3)) @ w_ff2
    return x + gate2 * f


def encoder_trunk(x, c, layer_params, final_g):
    for i in range(DEPTH):
        x = encoder_layer(x, c, *[p[i] for p in layer_params])
    return rmsnorm(x, final_g)


def setup_inputs(seed: int = 0) -> dict:
    key = jax.random.key(seed)
    ks = jax.random.split(key, 36)
    f32 = jnp.float32
    L, G, P, H = DEPTH, N_S5_GROUPS, S5_STATE, S5_GROUP

    def nrm(k, shape, scale):
        return scale * jax.random.normal(k, shape, f32)

    return {
        'x_prompt': nrm(ks[0], (BATCH, SEQ, D_MODEL), 1.0),
        'x_sample': nrm(ks[1], (DEC_BATCH, DEC_SEQ, D_MODEL), 1.0),
        'c_prompt': nrm(ks[2], (BATCH, D_MODEL), 1.0),
        'c_sample': nrm(ks[3], (DEC_BATCH, D_MODEL), 1.0),
        'norm1_g': 1.0 + nrm(ks[4], (L, D_MODEL), 0.02),
        'w_ada': nrm(ks[5], (L, D_MODEL, 6 * D_MODEL), 0.5 * D_MODEL ** -0.5),
        'b_ada': nrm(ks[6], (L, 6 * D_MODEL), 0.02),
        'w_in': nrm(ks[7], (L, D_MODEL, D_IN_PROJ), D_MODEL ** -0.5),
        'mu_shift': jax.random.uniform(ks[8], (L, RWKV_COLS), f32, 0.2, 0.8),
        'w0': jnp.broadcast_to(jnp.linspace(-6.0, -1.0, D_RWKV, dtype=f32), (L, N_DIR, D_RWKV))
              + nrm(ks[9], (L, N_DIR, D_RWKV), 0.1),
        'w2': nrm(ks[10], (L, N_DIR, DECAY_LORA, D_RWKV), 0.1 * DECAY_LORA ** -0.5),
        'a0': nrm(ks[11], (L, N_DIR, D_RWKV), 0.1),
        'a2': nrm(ks[12], (L, N_DIR, ICLR_LORA, D_RWKV), 0.5 * ICLR_LORA ** -0.5),
        'g2': nrm(ks[13], (L, GATE_LORA, D_RWKV), GATE_LORA ** -0.5),
        'k_k': 0.85 + nrm(ks[14], (L, D_RWKV), 0.02),
        'k_a': 1.0 + nrm(ks[15], (L, D_RWKV), 0.02),
        'r_k': -0.04 + nrm(ks[16], (L, N_HEADS, HEAD_DIM), 0.1),
        'lnx_g': 1.0 + nrm(ks[17], (L, D_RWKV), 0.02),
        'lnx_b': nrm(ks[18], (L, D_RWKV), 0.02),
        'lam_re': -0.5 + nrm(ks[19], (L, N_DIR, G, P), 0.01),
        'lam_im': jnp.pi * jnp.arange(P, dtype=f32) + nrm(ks[20], (L, N_DIR, G, P), 0.01),
        'log_dt': jax.random.uniform(ks[21], (L, N_DIR, G), f32, math.log(1e-3), math.log(1e-1)),
        'b_re': nrm(ks[22], (L, N_DIR, G, P, H), (2 * H) ** -0.5),
        'b_im': nrm(ks[23], (L, N_DIR, G, P, H), (2 * H) ** -0.5),
        'c_re': nrm(ks[24], (L, N_DIR, G, H, P), (2 * P) ** -0.5),
        'c_im': nrm(ks[25], (L, N_DIR, G, H, P), (2 * P) ** -0.5),
        'd_skip': nrm(ks[26], (L, G, H), 1.0),
        'w_glu': nrm(ks[27], (L, G, H, H), H ** -0.5),
        'b_glu': nrm(ks[28], (L, G, H), 0.02),
        's5_out_g': 1.0 + nrm(ks[29], (L, D_S5), 0.02),
        'w_out': nrm(ks[30], (L, D_MODEL, D_MODEL), D_MODEL ** -0.5),
        'norm2_g': 1.0 + nrm(ks[31], (L, D_MODEL), 0.02),
        'w_ff1': nrm(ks[32], (L, D_MODEL, D_FF), D_MODEL ** -0.5),
        'w_ff3': nrm(ks[33], (L, D_MODEL, D_FF), D_MODEL ** -0.5),
        'w_ff2': nrm(ks[34], (L, D_FF, D_MODEL), D_FF ** -0.5),
        'final_g': 1.0 + nrm(ks[35], (D_MODEL,), 0.02),
    }


def reference(x_prompt, x_sample, c_prompt, c_sample, norm1_g, w_ada, b_ada, w_in, mu_shift, w0, w2,
              a0, a2, g2, k_k, k_a, r_k, lnx_g, lnx_b, lam_re, lam_im, log_dt, b_re, b_im, c_re, c_im,
              d_skip, w_glu, b_glu, s5_out_g, w_out, norm2_g, w_ff1, w_ff3, w_ff2, final_g):
    layer_params = (norm1_g, w_ada, b_ada, w_in, mu_shift, w0, w2, a0, a2, g2, k_k, k_a, r_k,
                    lnx_g, lnx_b, lam_re, lam_im, log_dt, b_re, b_im, c_re, c_im, d_skip, w_glu,
                    b_glu, s5_out_g, w_out, norm2_g, w_ff1, w_ff3, w_ff2)
    y_prompt = encoder_trunk(x_prompt, c_prompt, layer_params, final_g)
    y_sample = encoder_trunk(x_sample, c_sample, layer_params, final_g)
    return (y_prompt, y_sample)
```

```python
import functools
import math

import jax
import jax.numpy as jnp
from jax import lax
from jax.experimental import pallas as pl
from jax.experimental.pallas import tpu as pltpu

F32 = jnp.float32
BF16 = jnp.bfloat16

D_MODEL = 1024
D_RWKV = 512
HEAD_DIM = 64
D_S5 = 512
S5_GROUP = 16
N_S5_GROUPS = 32
S5_STATE = 64
D_FF = 2816
RWKV_COLS = 1920
D_IN_PROJ = 2432
RMS_EPS = 1e-6
LNX_EPS = 64e-5

LANES = 128
CHUNK = 64
S5_CHUNK = 16
N_PAIRS = D_RWKV // LANES
N_S5_PAIRS = N_S5_GROUPS // 2
VMEM_LIMIT = 56 * 1024 * 1024


def _dot(a, b):
    return jnp.dot(a.astype(BF16), b.astype(BF16), preferred_element_type=F32)


def _dot_nt(a, b):
    return lax.dot_general(a.astype(BF16), b.astype(BF16), (((1,), (1,)), ((), ())),
                           preferred_element_type=F32)


def _dot_tn(a, b):
    return lax.dot_general(a.astype(BF16), b.astype(BF16), (((0,), (0,)), ((), ())),
                           preferred_element_type=F32)


def _split2(x):
    hi = x.astype(BF16)
    lo = (x - hi.astype(F32)).astype(BF16)
    return hi, lo


def _split3(x):
    hi = x.astype(BF16)
    r1 = x - hi.astype(F32)
    mid = r1.astype(BF16)
    lo = (r1 - mid.astype(F32)).astype(BF16)
    return hi, mid, lo


def _dot3(a, b):
    ah, al = _split2(a)
    bh, bl = _split2(b)
    d = functools.partial(jnp.dot, preferred_element_type=F32)
    return d(ah, bh) + (d(ah, bl) + d(al, bh))


def _dot_exact_lhs(m, x):
    mh = m.astype(BF16)
    h, mid, lo = _split3(x)
    d = functools.partial(jnp.dot, preferred_element_type=F32)
    return d(mh, h) + (d(mh, mid) + d(mh, lo))


def _dot_exact_rhs(x, m):
    mh = m.astype(BF16)
    h, mid, lo = _split3(x)
    d = functools.partial(jnp.dot, preferred_element_type=F32)
    return d(h, mh) + (d(mid, mh) + d(lo, mh))


def _sigmoid(x):
    return 1.0 / (1.0 + jnp.exp(-x))


def _softplus(x):
    return jnp.maximum(x, 0.0) + jnp.log1p(jnp.exp(-jnp.abs(x)))


def _const_spec(shape):
    nd = len(shape)
    return pl.BlockSpec(shape, lambda *_: (0,) * nd)


def _mod_kernel(c_ref, w_ref, b_ref, o_ref):
    c = c_ref[...]
    s = c * _sigmoid(c)
    o_ref[...] = _dot(s, w_ref[...]) + b_ref[...]


def _mod_call(c_all, w_ada, b_ada):
    nb = c_all.shape[0]
    ncol = w_ada.shape[1] // D_MODEL
    return pl.pallas_call(
        _mod_kernel,
        grid=(ncol,),
        in_specs=[
            pl.BlockSpec((nb, D_MODEL), lambda j: (0, 0)),
            pl.BlockSpec((D_MODEL, D_MODEL), lambda j: (0, j)),
            pl.BlockSpec((1, D_MODEL), lambda j: (0, j)),
        ],
        out_specs=pl.BlockSpec((nb, D_MODEL), lambda j: (0, j)),
        out_shape=jax.ShapeDtypeStruct((nb, w_ada.shape[1]), F32),
        name="adaln_mod",
    )(c_all, w_ada, b_ada)


HALO = 8


def _in_kernel(x_ref, xp_ref, xn_ref, mod_ref, g_ref, w_ref, mu_ref, ps_ref, u_ref, p_scr, *, tb, nt):
    i = pl.program_id(1)
    shift = mod_ref[0:1, :]
    scale = mod_ref[1:2, :]
    g = g_ref[...]
    xa = jnp.concatenate([xp_ref[...], x_ref[...], xn_ref[...]], axis=0)
    ms = jnp.mean(xa * xa, axis=-1, keepdims=True)
    h = ((xa * lax.rsqrt(ms + RMS_EPS)) * g) * (1.0 + scale) + shift
    hb = h.astype(BF16)
    p_scr[...] = jnp.dot(hb, w_ref[:, :RWKV_COLS], preferred_element_type=F32)
    u_ref[...] = jnp.dot(hb, w_ref[:, RWKV_COLS:], preferred_element_type=F32)[HALO:HALO + tb]
    p = p_scr[HALO:HALO + tb, :]
    prev = p_scr[HALO - 1:HALO - 1 + tb, :]
    nxt = p_scr[HALO + 1:HALO + 1 + tb, :]
    row = lax.broadcasted_iota(jnp.int32, (tb, 1), 0)
    prev = jnp.where(jnp.logical_and(row == 0, i == 0), 0.0, prev)
    nxt = jnp.where(jnp.logical_and(row == tb - 1, i == nt - 1), 0.0, nxt)
    ps_ref[...] = p + mu_ref[...] * (0.5 * (prev + nxt) - p)


def _in_call(x, mod3, norm1_g, w_in, mu_shift, tb):
    bsz, t_len, _ = x.shape
    nt = t_len // tb
    hb = tb // HALO
    last = t_len // HALO - 1
    kern = functools.partial(_in_kernel, tb=tb, nt=nt)
    return pl.pallas_call(
        kern,
        grid=(bsz, nt),
        in_specs=[
            pl.BlockSpec((None, tb, D_MODEL), lambda b, i: (b, i, 0)),
            pl.BlockSpec((None, HALO, D_MODEL), lambda b, i: (b, jnp.maximum(i * hb - 1, 0), 0)),
            pl.BlockSpec((None, HALO, D_MODEL), lambda b, i: (b, jnp.minimum((i + 1) * hb, last), 0)),
            pl.BlockSpec((None, 6, D_MODEL), lambda b, i: (b, 0, 0)),
            _const_spec((1, D_MODEL)),
            pl.BlockSpec((D_MODEL, D_IN_PROJ), lambda b, i: (0, 0), pipeline_mode=pl.Buffered(1)),
            _const_spec((1, RWKV_COLS)),
        ],
        out_specs=[
            pl.BlockSpec((None, tb, RWKV_COLS), lambda b, i: (b, i, 0)),
            pl.BlockSpec((None, tb, D_S5), lambda b, i: (b, i, 0)),
        ],
        out_shape=[
            jax.ShapeDtypeStruct((bsz, t_len, RWKV_COLS), F32),
            jax.ShapeDtypeStruct((bsz, t_len, D_S5), F32),
        ],
        scratch_shapes=[pltpu.VMEM((tb + 2 * HALO, RWKV_COLS), F32)],
        compiler_params=pltpu.CompilerParams(vmem_limit_bytes=VMEM_LIMIT),
        name="in_proj_shift",
    )(x, x, x, mod3, norm1_g, w_in, mu_shift)


def _head_sum(x, blk):
    h, lo = _split2(x)
    d = functools.partial(jnp.dot, preferred_element_type=F32)
    return d(h, blk) + d(lo, blk)


def _tri_inverse(n_raw, ri, ci, reverse):
    before = (ci > ri) if reverse else (ci < ri)
    eye = (ri == ci).astype(F32)
    m1 = jnp.logical_and((ri >> 1) == (ci >> 1), before)
    t = eye + jnp.where(m1, n_raw, 0.0)
    sh = 1
    while (1 << sh) < CHUNK:
        mk = jnp.logical_and(jnp.logical_and((ri >> (sh + 1)) == (ci >> (sh + 1)),
                                             (ri >> sh) != (ci >> sh)), before)
        c = jnp.where(mk, n_raw, 0.0)
        t = t + _dot3(_dot3(t, c), t)
        sh += 1
    return t


def _scan_kernel(*refs, tbs, reverse, finalize):
    if finalize:
        (r_ref, k_ref, v_ref, xw_ref, xa_ref, xg_ref, w0_ref, w2_ref, a0_ref, a2_ref, kk_ref, ka_ref,
         rk_ref, g2_ref, lg_ref, lb_ref, a0o_ref, a2o_ref, yo_ref,
         y_ref, s_scr, lw_scr, a_scr, kd_scr, kkn_scr) = refs
    else:
        (r_ref, k_ref, v_ref, xw_ref, xa_ref, xg_ref, w0_ref, w2_ref, a0_ref, a2_ref, kk_ref, ka_ref,
         y_ref, s_scr, lw_scr, a_scr, kd_scr, kkn_scr) = refs

    @pl.when(pl.program_id(2) == 0)
    def _():
        s_scr[...] = jnp.zeros_like(s_scr)

    two = 2 * CHUNK
    ri = lax.broadcasted_iota(jnp.int32, (two, two), 0)
    ci = lax.broadcasted_iota(jnp.int32, (two, two), 1)
    same_head = (ri >> 6) == (ci >> 6)
    blk = same_head.astype(BF16)
    before = (ci > ri) if reverse else (ci < ri)
    bd_strict = jnp.logical_and(same_head, before)
    bd_incl = jnp.logical_and(same_head, jnp.logical_or(before, ri == ci))
    lane = lax.broadcasted_iota(jnp.int32, (CHUNK, LANES), 1)
    head0 = lane < HEAD_DIM
    r64 = lax.broadcasted_iota(jnp.int32, (CHUNK, CHUNK), 0)
    c64 = lax.broadcasted_iota(jnp.int32, (CHUNK, CHUNK), 1)
    tri = ((c64 >= r64) if reverse else (c64 <= r64)).astype(F32)

    def stack(x):
        return jnp.concatenate([jnp.where(head0, x, 0.0), jnp.where(head0, 0.0, x)], axis=0)

    k = k_ref[...]
    zw = w0_ref[...] + _dot(jnp.tanh(xw_ref[...]), w2_ref[...])
    wd = -_softplus(-zw) - 0.5
    lw_scr[...] = -jnp.exp(wd)
    a = _sigmoid(a0_ref[...] + _dot(xa_ref[...], a2_ref[...]))
    a_scr[...] = a
    ka = ka_ref[...]
    kd = k * (1.0 + (a - 1.0) * ka)
    kd_scr[...] = kd
    kkr = k * kk_ref[...]
    kkn_scr[...] = kkr * lax.rsqrt(jnp.maximum(_head_sum(kkr * kkr, blk), 1e-24))

    n_chunks = tbs // CHUNK
    order = range(n_chunks - 1, -1, -1) if reverse else range(n_chunks)
    for ch in order:
        sl = pl.ds(ch * CHUNK, CHUNK)
        r = r_ref[sl, :]
        v = v_ref[sl, :]
        kdc = kd_scr[sl, :]
        ac = a_scr[sl, :]
        kk = kkn_scr[sl, :]
        lw = lw_scr[sl, :]
        c = _dot_exact_lhs(tri, lw)
        ctot = c[0:1, :] if reverse else c[CHUNK - 1:CHUNK, :]
        eg = jnp.exp(c)
        ei = jnp.exp(-c)
        ex = jnp.exp(c - lw)
        ee = jnp.exp(ctot - c)
        akk = ac * kk
        rt_s = stack(r * eg)
        at_s = stack(-(kk * ex))
        bt = akk * ei
        kt = kdc * ei
        bte_s = stack(akk * ee)
        kte_s = stack(kdc * ee)
        v_s = stack(v)

        x_l = jnp.concatenate([at_s, rt_s], axis=0)
        y_r = jnp.concatenate([bt, bt, kt, kt], axis=0)
        g = _dot_nt(x_l, y_r)
        n_raw = g[:two, :two]
        aak = jnp.where(bd_strict, g[:two, two:], 0.0)
        arb = jnp.where(bd_incl, g[two:, :two], 0.0)
        ark = jnp.where(bd_incl, g[two:, two:], 0.0)

        t_inv = _tri_inverse(n_raw, ri, ci, reverse)
        w_mat = _dot3(t_inv, at_s)
        u_loc = _dot3(t_inv, _dot(aak, v_s))
        y_loc = _dot(ark, v_s)
        kv = _dot_tn(v_s, kte_s)

        s = s_scr[...]
        u = _dot_nt(w_mat, s) + u_loc
        y_s = _dot_nt(rt_s, s) + _dot(arb, u) + y_loc
        y_ref[sl, :] = y_s[:CHUNK] + y_s[CHUNK:]
        s_scr[...] = s * jnp.exp(ctot) + _dot_tn(u, bte_s) + kv

    if finalize:
        y = y_ref[...] + yo_ref[...]
        inv_n = 1.0 / HEAD_DIM
        mean = _head_sum(y, blk) * inv_n
        d = y - mean
        var = _head_sum(d * d, blk) * inv_n
        yn = d * lax.rsqrt(var + LNX_EPS) * lg_ref[...] + lb_ref[...]
        a_o = _sigmoid(a0o_ref[...] + _dot(xa_ref[...], a2o_ref[...]))
        kd_o = k * (1.0 + (a_o - 1.0) * ka)
        r_all = r_ref[...]
        bonus = (_head_sum(r_all * kd * rk_ref[...], blk) + _head_sum(r_all * kd_o * rk_ref[...], blk)) * v_ref[...]
        gate = _dot(_sigmoid(xg_ref[...]), g2_ref[...])
        y_ref[...] = (yn + bonus) * gate


def _scan_call(ps, vecs, mats, y_other, *, direction, tbs):
    bsz, t_len, _ = ps.shape
    nt = t_len // tbs
    reverse = direction == 1
    finalize = y_other is not None

    def tmap(i):
        return (nt - 1 - i) if reverse else i

    def col_spec(col):
        return pl.BlockSpec((None, tbs, LANES), lambda b, p, i: (b, tmap(i), col(p)))

    def vec_spec():
        return pl.BlockSpec((1, LANES), lambda b, p, i: (0, p))

    def mat_spec():
        return pl.BlockSpec((LANES, LANES), lambda b, p, i: (0, p))

    in_specs = [
        col_spec(lambda p: p), col_spec(lambda p: N_PAIRS + p), col_spec(lambda p: 2 * N_PAIRS + p),
        col_spec(lambda p: 3 * N_PAIRS), col_spec(lambda p: 3 * N_PAIRS + 1), col_spec(lambda p: 3 * N_PAIRS + 2),
        vec_spec(), mat_spec(), vec_spec(), mat_spec(), vec_spec(), vec_spec(),
    ]
    o = 1 - direction
    args = [ps] * 6 + [vecs["w0"][direction], mats["w2p"][direction], vecs["a0"][direction],
                       mats["a2p"][direction], vecs["k_k"], vecs["k_a"]]
    if finalize:
        in_specs += [vec_spec(), mat_spec(), vec_spec(), vec_spec(), vec_spec(), mat_spec(),
                     pl.BlockSpec((None, tbs, LANES), lambda b, p, i: (b, tmap(i), p))]
        args += [vecs["r_k"], mats["g2"], vecs["lnx_g"], vecs["lnx_b"], vecs["a0"][o], mats["a2p"][o], y_other]
    kern = functools.partial(_scan_kernel, tbs=tbs, reverse=reverse, finalize=finalize)
    return pl.pallas_call(
        kern,
        grid=(bsz, N_PAIRS, nt),
        in_specs=in_specs,
        out_specs=pl.BlockSpec((None, tbs, LANES), lambda b, p, i: (b, tmap(i), p)),
        out_shape=jax.ShapeDtypeStruct((bsz, t_len, D_RWKV), F32),
        scratch_shapes=[pltpu.VMEM((2 * CHUNK, LANES), F32)] + [pltpu.VMEM((tbs, LANES), F32)] * 4,
        compiler_params=pltpu.CompilerParams(
            dimension_semantics=("arbitrary", "arbitrary", "arbitrary"), vmem_limit_bytes=VMEM_LIMIT),
        name="rwkv7_scan_bwd" if reverse else "rwkv7_scan_fwd",
    )(*args)


def _gelu_tanh(x):
    return 0.5 * x * (1.0 + jnp.tanh(math.sqrt(2.0 / math.pi) * (x + 0.044715 * (x * x * x))))


def _s5_kernel(u_ref, wz_ref, tz_ref, cz_ref, pw_ref, d_ref, wg_ref, bg_ref, o_ref, *, nc, nlev):
    u = u_ref[...]
    ub = u.astype(BF16)
    z = jnp.dot(ub, wz_ref[...], preferred_element_type=F32)
    row = lax.broadcasted_iota(jnp.int32, (nc, LANES), 0)
    xs = [z[:, j * LANES:(j + 1) * LANES] for j in range(4)]

    def shifted(x, s, up):
        if up:
            return jnp.where(row >= nc - s, 0.0, pltpu.roll(x, nc - s, 0))
        return jnp.where(row < s, 0.0, pltpu.roll(x, s, 0))

    for lev in range(nlev):
        s = 1 << lev
        for d in range(2):
            xr, xi = xs[2 * d], xs[2 * d + 1]
            ar = pw_ref[lev, 2 * d:2 * d + 1, :]
            ai = pw_ref[lev, 2 * d + 1:2 * d + 2, :]
            pr = shifted(xr, s, d == 1)
            pi = shifted(xi, s, d == 1)
            xs[2 * d] = xr + (ar * pr - ai * pi)
            xs[2 * d + 1] = xi + (ar * pi + ai * pr)
    xin = jnp.concatenate([shifted(xs[0], 1, False), shifted(xs[1], 1, False),
                           shifted(xs[2], 1, True), shifted(xs[3], 1, True)], axis=1)
    y = (jnp.dot(ub, tz_ref[...], preferred_element_type=F32)
         + jnp.dot(xin.astype(BF16), cz_ref[...], preferred_element_type=F32)
         + d_ref[...] * u)
    zact = _gelu_tanh(y)
    gate = _sigmoid(jnp.dot(zact.astype(BF16), wg_ref[...], preferred_element_type=F32) + bg_ref[...])
    o_ref[...] = zact * gate


def _s5_call(ug, tabs):
    bsz, npair, nc, width = ug.shape
    nlev = max(1, (nc - 1).bit_length())
    kern = functools.partial(_s5_kernel, nc=nc, nlev=nlev)

    def mat_spec():
        return pl.BlockSpec((None, width, width), lambda b, p: (p, 0, 0))

    return pl.pallas_call(
        kern,
        grid=(bsz, npair),
        in_specs=[
            pl.BlockSpec((None, None, nc, width), lambda b, p: (b, p, 0, 0)),
            mat_spec(), mat_spec(), mat_spec(),
            pl.BlockSpec((None,) + tabs["pw"].shape[1:], lambda b, p: (p, 0, 0, 0)),
            pl.BlockSpec((None, 1, width), lambda b, p: (p, 0, 0)),
            mat_spec(),
            pl.BlockSpec((None, 1, width), lambda b, p: (p, 0, 0)),
        ],
        out_specs=pl.BlockSpec((None, None, nc, width), lambda b, p: (b, p, 0, 0)),
        out_shape=jax.ShapeDtypeStruct(ug.shape, F32),
        compiler_params=pltpu.CompilerParams(vmem_limit_bytes=VMEM_LIMIT),
        name="s5_mixer",
    )(ug, tabs["wz"], tabs["tz"], tabs["cz"], tabs["pw"], tabs["dvec"], tabs["wglu"], tabs["bglu"])


def _s5_tables(lam_re, lam_im, log_dt, b_re, b_im, c_re, c_im, d_skip, w_glu, b_glu, max_levels):
    hp = lax.Precision.HIGHEST
    L, G, P, H = S5_CHUNK, N_S5_GROUPS, S5_STATE, S5_GROUP
    dt = jnp.exp(log_dt)[:, :, None]
    are, aim = lam_re * dt, lam_im * dt

    def apow(n):
        n = jnp.asarray(n, F32)
        shape = n.shape + (1, 1, 1)
        mag = jnp.exp(n.reshape(shape) * are)
        ang = n.reshape(shape) * aim
        return mag * jnp.cos(ang), mag * jnp.sin(ang)

    a1r, a1i = apow(jnp.ones(()))
    den = lam_re * lam_re + lam_im * lam_im
    qr = ((a1r - 1.0) * lam_re + a1i * lam_im) / den
    qi = (a1i * lam_re - (a1r - 1.0) * lam_im) / den
    bbr = qr[..., None] * b_re - qi[..., None] * b_im
    bbi = qr[..., None] * b_im + qi[..., None] * b_re
    pr, pi = apow(jnp.arange(L + 1))

    abr = pr[..., None] * bbr - pi[..., None] * bbi
    abi = pr[..., None] * bbi + pi[..., None] * bbr
    s_idx = jnp.arange(L)
    wf_r = jnp.transpose(abr[L - 1 - s_idx, 0], (1, 0, 3, 2))
    wf_i = jnp.transpose(abi[L - 1 - s_idx, 0], (1, 0, 3, 2))
    wb_r = jnp.transpose(abr[s_idx, 1], (1, 0, 3, 2))
    wb_i = jnp.transpose(abi[s_idx, 1], (1, 0, 3, 2))
    kf = (jnp.einsum("ghp,ngpi->nghi", c_re[0], abr[:L, 0], precision=hp)
          - jnp.einsum("ghp,ngpi->nghi", c_im[0], abi[:L, 0], precision=hp))
    kb = (jnp.einsum("ghp,ngpi->nghi", c_re[1], abr[:L, 1], precision=hp)
          - jnp.einsum("ghp,ngpi->nghi", c_im[1], abi[:L, 1], precision=hp))
    lag = s_idx[None, :] - s_idx[:, None]
    tf = jnp.where((lag >= 0)[:, :, None, None, None], kf[jnp.clip(lag, 0, L - 1)], 0.0)
    tb = jnp.where((lag <= 0)[:, :, None, None, None], kb[jnp.clip(-lag, 0, L - 1)], 0.0)
    tz_g = jnp.transpose(tf + tb, (2, 0, 4, 1, 3)).reshape(G, L * H, L * H)
    t1 = s_idx + 1
    cf_r = (c_re[0][:, None] * pr[t1, 0].transpose(1, 0, 2)[:, :, None, :]
            - c_im[0][:, None] * pi[t1, 0].transpose(1, 0, 2)[:, :, None, :])
    cf_i = (c_re[0][:, None] * pi[t1, 0].transpose(1, 0, 2)[:, :, None, :]
            + c_im[0][:, None] * pr[t1, 0].transpose(1, 0, 2)[:, :, None, :])
    t2 = L - s_idx
    cb_r = (c_re[1][:, None] * pr[t2, 1].transpose(1, 0, 2)[:, :, None, :]
            - c_im[1][:, None] * pi[t2, 1].transpose(1, 0, 2)[:, :, None, :])
    cb_i = (c_re[1][:, None] * pi[t2, 1].transpose(1, 0, 2)[:, :, None, :]
            + c_im[1][:, None] * pr[t2, 1].transpose(1, 0, 2)[:, :, None, :])

    def rows_p(m):
        return jnp.transpose(m, (0, 3, 1, 2)).reshape(G, P, L * H)

    def cols_p(m):
        return m.reshape(G, L * H, P)

    def pair_blockdiag(m):
        g2, r, c = m.shape[0] // 2, m.shape[1], m.shape[2]
        m = m.reshape(g2, 2, r, c)
        z = jnp.zeros((g2, r, c), m.dtype)
        top = jnp.concatenate([m[:, 0], z], axis=2)
        bot = jnp.concatenate([z, m[:, 1]], axis=2)
        return jnp.concatenate([top, bot], axis=1)

    wz = jnp.concatenate([pair_blockdiag(cols_p(m)) for m in (wf_r, wf_i, wb_r, wb_i)], axis=2)
    cz = jnp.concatenate([pair_blockdiag(rows_p(m)) for m in (cf_r, -cf_i, cb_r, -cb_i)], axis=1)
    tz = pair_blockdiag(tz_g)
    eye_l = jnp.eye(L, dtype=F32)
    wglu_g = jnp.einsum("st,ghk->gshtk", eye_l, w_glu).reshape(G, L * H, L * H)
    wglu = pair_blockdiag(wglu_g)

    def lanes_gh(vec):
        return jnp.broadcast_to(vec[:, None, :], (G, L, H)).reshape(G // 2, 1, 2 * L * H)

    lev_n = (L * (2 ** jnp.arange(max_levels))).astype(F32)
    lr, li = apow(lev_n)

    def lanes_gp(m):
        return jnp.transpose(m.reshape(max_levels, G // 2, 2 * P), (1, 0, 2))[:, :, None, :]

    pw = jnp.concatenate([lanes_gp(lr[:, 0]), lanes_gp(li[:, 0]), lanes_gp(lr[:, 1]), lanes_gp(li[:, 1])],
                         axis=2)
    return {
        "wz": wz.astype(BF16), "tz": tz.astype(BF16), "cz": cz.astype(BF16), "pw": pw,
        "dvec": lanes_gh(d_skip), "wglu": wglu.astype(BF16), "bglu": lanes_gh(b_glu),
    }


def _rms(x):
    return x * lax.rsqrt(jnp.mean(x * x, axis=-1, keepdims=True) + RMS_EPS)


def _out_kernel(x_ref, yr_ref, zs_ref, mod_ref, sg_ref, wo_ref, n2_ref, w1_ref, w3_ref, w2_ref, fg_ref, o_ref):
    gate1 = mod_ref[2:3, :]
    shift2 = mod_ref[3:4, :]
    scale2 = mod_ref[4:5, :]
    gate2 = mod_ref[5:6, :]
    ys = _rms(zs_ref[...]) * sg_ref[...]
    mix = (jnp.dot(yr_ref[...].astype(BF16), wo_ref[:D_RWKV, :], preferred_element_type=F32)
           + jnp.dot(ys.astype(BF16), wo_ref[D_RWKV:, :], preferred_element_type=F32))
    x1 = x_ref[...] + gate1 * mix
    h = ((_rms(x1) * n2_ref[...]) * (1.0 + scale2) + shift2).astype(BF16)
    f1 = jnp.dot(h, w1_ref[...], preferred_element_type=F32)
    f3 = jnp.dot(h, w3_ref[...], preferred_element_type=F32)
    act = (f1 * _sigmoid(f1)) * f3
    f = jnp.dot(act.astype(BF16), w2_ref[...], preferred_element_type=F32)
    x2 = x1 + gate2 * f
    o_ref[...] = _rms(x2) * fg_ref[...]


def _out_call(x, y_rwkv, z_s5, mod3, s5_out_g, w_out, norm2_g, w_ff1, w_ff3, w_ff2, final_g, tb):
    bsz, t_len, _ = x.shape

    def resident(shape):
        return pl.BlockSpec(shape, lambda b, i: (0, 0), pipeline_mode=pl.Buffered(1))

    return pl.pallas_call(
        _out_kernel,
        grid=(bsz, t_len // tb),
        in_specs=[
            pl.BlockSpec((None, tb, D_MODEL), lambda b, i: (b, i, 0)),
            pl.BlockSpec((None, tb, D_RWKV), lambda b, i: (b, i, 0)),
            pl.BlockSpec((None, tb, D_S5), lambda b, i: (b, i, 0)),
            pl.BlockSpec((None, 6, D_MODEL), lambda b, i: (b, 0, 0)),
            _const_spec((1, D_S5)),
            resident((D_MODEL, D_MODEL)),
            _const_spec((1, D_MODEL)),
            resident((D_MODEL, D_FF)),
            resident((D_MODEL, D_FF)),
            resident((D_FF, D_MODEL)),
            _const_spec((1, D_MODEL)),
        ],
        out_specs=pl.BlockSpec((None, tb, D_MODEL), lambda b, i: (b, i, 0)),
        out_shape=jax.ShapeDtypeStruct(x.shape, F32),
        compiler_params=pltpu.CompilerParams(vmem_limit_bytes=VMEM_LIMIT),
        name="out_proj_ffn",
    )(x, y_rwkv, z_s5, mod3, s5_out_g, w_out, norm2_g, w_ff1, w_ff3, w_ff2, final_g)


def _block_sizes(t_len):
    tb_in = min(512, t_len)
    tbs = min(256, t_len)
    tb_out = min(256, t_len)
    return tb_in, tbs, tb_out


def _pad_lora(w, n_in):
    z = jnp.zeros_like(w[0])
    return [jnp.concatenate([w[0], z], axis=0).astype(BF16), jnp.concatenate([z, w[1]], axis=0).astype(BF16)]


def _trunk(x, mod3, prm, s5_tabs):
    bsz, t_len, _ = x.shape
    tb_in, tbs, tb_out = _block_sizes(t_len)
    ps, u = _in_call(x, mod3, prm["norm1_g"], prm["w_in"], prm["mu_shift"], tb_in)
    y_b = _scan_call(ps, prm["vecs"], prm["mats"], None, direction=1, tbs=tbs)
    y_rwkv = _scan_call(ps, prm["vecs"], prm["mats"], y_b, direction=0, tbs=tbs)
    nc = t_len // S5_CHUNK
    ug = u.reshape(bsz, nc, S5_CHUNK, N_S5_PAIRS, 2, S5_GROUP)
    ug = jnp.transpose(ug, (0, 3, 1, 4, 2, 5)).reshape(bsz, N_S5_PAIRS, nc, 2 * S5_CHUNK * S5_GROUP)
    zg = _s5_call(ug, s5_tabs)
    zg = zg.reshape(bsz, N_S5_PAIRS, nc, 2, S5_CHUNK, S5_GROUP)
    z_s5 = jnp.transpose(zg, (0, 2, 4, 1, 3, 5)).reshape(bsz, t_len, D_S5)
    return _out_call(x, y_rwkv, z_s5, mod3, prm["s5_out_g"], prm["w_out"], prm["norm2_g"],
                     prm["w_ff1"], prm["w_ff3"], prm["w_ff2"], prm["final_g"], tb_out)


def _prepare(norm1_g, w_in, mu_shift, w0, w2, a0, a2, g2, k_k, k_a, r_k, lnx_g, lnx_b, s5_out_g, w_out,
             norm2_g, w_ff1, w_ff3, w_ff2, final_g):
    row = lambda v: v.reshape(1, -1)
    vecs = {
        "w0": [row(w0[0]), row(w0[1])], "a0": [row(a0[0]), row(a0[1])],
        "k_k": row(k_k), "k_a": row(k_a), "r_k": row(r_k), "lnx_g": row(lnx_g), "lnx_b": row(lnx_b),
    }
    mats = {"w2p": _pad_lora(w2, 64), "a2p": _pad_lora(a2, 64), "g2": g2.astype(BF16)}
    return {
        "norm1_g": row(norm1_g), "w_in": w_in.astype(BF16), "mu_shift": row(mu_shift),
        "vecs": vecs, "mats": mats, "s5_out_g": row(s5_out_g), "w_out": w_out.astype(BF16),
        "norm2_g": row(norm2_g), "w_ff1": w_ff1.astype(BF16), "w_ff3": w_ff3.astype(BF16),
        "w_ff2": w_ff2.astype(BF16), "final_g": row(final_g),
    }


def kernel(x_prompt, x_sample, c_prompt, c_sample, norm1_g, w_ada, b_ada, w_in, mu_shift, w0, w2, a0, a2, g2,
           k_k, k_a, r_k, lnx_g, lnx_b, lam_re, lam_im, log_dt, b_re, b_im, c_re, c_im, d_skip, w_glu, b_glu,
           s5_out_g, w_out, norm2_g, w_ff1, w_ff3, w_ff2, final_g):
    depth = norm1_g.shape[0]
    assert depth == 1, "the fused output kernel applies the final norm, so it supports a single layer"
    nbp = x_prompt.shape[0]
    xs = [x_prompt, x_sample]
    c_all = jnp.concatenate([c_prompt, c_sample], axis=0)
    max_t = max(x_prompt.shape[1], x_sample.shape[1])
    max_levels = max(1, (max_t // S5_CHUNK - 1).bit_length())
    for i in range(depth):
        prm = _prepare(norm1_g[i], w_in[i], mu_shift[i], w0[i], w2[i], a0[i], a2[i], g2[i], k_k[i], k_a[i],
                       r_k[i], lnx_g[i], lnx_b[i], s5_out_g[i], w_out[i], norm2_g[i], w_ff1[i], w_ff3[i],
                       w_ff2[i], final_g)
        s5_tabs = _s5_tables(lam_re[i], lam_im[i], log_dt[i], b_re[i], b_im[i], c_re[i], c_im[i],
                             d_skip[i], w_glu[i], b_glu[i], max_levels)
        mod = _mod_call(c_all, w_ada[i].astype(BF16), b_ada[i].reshape(1, -1))
        mod3 = mod.reshape(c_all.shape[0], 6, D_MODEL)
        xs = [_trunk(xs[0], mod3[:nbp], prm, s5_tabs), _trunk(xs[1], mod3[nbp:], prm, s5_tabs)]
    return tuple(xs)
```

```python
import functools
import math

import jax
import jax.numpy as jnp
from jax import lax
from jax.experimental import pallas as pl
from jax.experimental.pallas import tpu as pltpu

F32 = jnp.float32
BF16 = jnp.bfloat16

D_MODEL = 1024
D_RWKV = 512
HEAD_DIM = 64
D_S5 = 512
S5_GROUP = 16
N_S5_GROUPS = 32
S5_STATE = 64
D_FF = 2816
RWKV_COLS = 1920
D_IN_PROJ = 2432
RMS_EPS = 1e-6
LNX_EPS = 64e-5

LANES = 128
CHUNK = 64
S5_CHUNK = 16
N_PAIRS = D_RWKV // LANES
N_S5_PAIRS = N_S5_GROUPS // 2
VMEM_LIMIT = 56 * 1024 * 1024


def _dot(a, b):
    return jnp.dot(a.astype(BF16), b.astype(BF16), preferred_element_type=F32)


def _dot_nt(a, b):
    return lax.dot_general(a.astype(BF16), b.astype(BF16), (((1,), (1,)), ((), ())),
                           preferred_element_type=F32)


def _dot_tn(a, b):
    return lax.dot_general(a.astype(BF16), b.astype(BF16), (((0,), (0,)), ((), ())),
                           preferred_element_type=F32)


def _split2(x):
    hi = x.astype(BF16)
    lo = (x - hi.astype(F32)).astype(BF16)
    return hi, lo


def _split3(x):
    hi = x.astype(BF16)
    r1 = x - hi.astype(F32)
    mid = r1.astype(BF16)
    lo = (r1 - mid.astype(F32)).astype(BF16)
    return hi, mid, lo


def _dot3(a, b):
    ah, al = _split2(a)
    bh, bl = _split2(b)
    d = functools.partial(jnp.dot, preferred_element_type=F32)
    return d(ah, bh) + (d(ah, bl) + d(al, bh))


def _dot_exact_lhs(m, x):
    mh = m.astype(BF16)
    h, mid, lo = _split3(x)
    d = functools.partial(jnp.dot, preferred_element_type=F32)
    return d(mh, h) + (d(mh, mid) + d(mh, lo))


def _dot_exact_rhs(x, m):
    mh = m.astype(BF16)
    h, mid, lo = _split3(x)
    d = functools.partial(jnp.dot, preferred_element_type=F32)
    return d(h, mh) + (d(mid, mh) + d(lo, mh))


def _sigmoid(x):
    return 1.0 / (1.0 + jnp.exp(-x))


def _softplus(x):
    return jnp.maximum(x, 0.0) + jnp.log1p(jnp.exp(-jnp.abs(x)))


def _const_spec(shape):
    nd = len(shape)
    return pl.BlockSpec(shape, lambda *_: (0,) * nd)


def _mod_kernel(c_ref, w_ref, b_ref, o_ref):
    c = c_ref[...]
    s = c * _sigmoid(c)
    o_ref[...] = _dot(s, w_ref[...]) + b_ref[...]


def _mod_call(c_all, w_ada, b_ada):
    nb = c_all.shape[0]
    ncol = w_ada.shape[1] // D_MODEL
    return pl.pallas_call(
        _mod_kernel,
        grid=(ncol,),
        in_specs=[
            pl.BlockSpec((nb, D_MODEL), lambda j: (0, 0)),
            pl.BlockSpec((D_MODEL, D_MODEL), lambda j: (0, j)),
            pl.BlockSpec((1, D_MODEL), lambda j: (0, j)),
        ],
        out_specs=pl.BlockSpec((nb, D_MODEL), lambda j: (0, j)),
        out_shape=jax.ShapeDtypeStruct((nb, w_ada.shape[1]), F32),
        name="adaln_mod",
    )(c_all, w_ada, b_ada)


HALO = 8


def _in_kernel(x_ref, xp_ref, xn_ref, mod_ref, g_ref, w_ref, mu_ref, ps_ref, u_ref, p_scr, *, tb, nt):
    i = pl.program_id(1)
    shift = mod_ref[0:1, :]
    scale = mod_ref[1:2, :]
    g = g_ref[...]
    xa = jnp.concatenate([xp_ref[...], x_ref[...], xn_ref[...]], axis=0)
    ms = jnp.mean(xa * xa, axis=-1, keepdims=True)
    h = ((xa * lax.rsqrt(ms + RMS_EPS)) * g) * (1.0 + scale) + shift
    hb = h.astype(BF16)
    p_scr[...] = jnp.dot(hb, w_ref[:, :RWKV_COLS], preferred_element_type=F32)
    u_ref[...] = jnp.dot(hb, w_ref[:, RWKV_COLS:], preferred_element_type=F32)[HALO:HALO + tb]
    p = p_scr[HALO:HALO + tb, :]
    prev = p_scr[HALO - 1:HALO - 1 + tb, :]
    nxt = p_scr[HALO + 1:HALO + 1 + tb, :]
    row = lax.broadcasted_iota(jnp.int32, (tb, 1), 0)
    prev = jnp.where(jnp.logical_and(row == 0, i == 0), 0.0, prev)
    nxt = jnp.where(jnp.logical_and(row == tb - 1, i == nt - 1), 0.0, nxt)
    ps_ref[...] = p + mu_ref[...] * (0.5 * (prev + nxt) - p)


def _in_call(x, mod3, norm1_g, w_in, mu_shift, tb):
    bsz, t_len, _ = x.shape
    nt = t_len // tb
    hb = tb // HALO
    last = t_len // HALO - 1
    kern = functools.partial(_in_kernel, tb=tb, nt=nt)
    return pl.pallas_call(
        kern,
        grid=(bsz, nt),
        in_specs=[
            pl.BlockSpec((None, tb, D_MODEL), lambda b, i: (b, i, 0)),
            pl.BlockSpec((None, HALO, D_MODEL), lambda b, i: (b, jnp.maximum(i * hb - 1, 0), 0)),
            pl.BlockSpec((None, HALO, D_MODEL), lambda b, i: (b, jnp.minimum((i + 1) * hb, last), 0)),
            pl.BlockSpec((None, 6, D_MODEL), lambda b, i: (b, 0, 0)),
            _const_spec((1, D_MODEL)),
            pl.BlockSpec((D_MODEL, D_IN_PROJ), lambda b, i: (0, 0), pipeline_mode=pl.Buffered(1)),
            _const_spec((1, RWKV_COLS)),
        ],
        out_specs=[
            pl.BlockSpec((None, tb, RWKV_COLS), lambda b, i: (b, i, 0)),
            pl.BlockSpec((None, tb, D_S5), lambda b, i: (b, i, 0)),
        ],
        out_shape=[
            jax.ShapeDtypeStruct((bsz, t_len, RWKV_COLS), F32),
            jax.ShapeDtypeStruct((bsz, t_len, D_S5), F32),
        ],
        scratch_shapes=[pltpu.VMEM((tb + 2 * HALO, RWKV_COLS), F32)],
        compiler_params=pltpu.CompilerParams(vmem_limit_bytes=VMEM_LIMIT),
        name="in_proj_shift",
    )(x, x, x, mod3, norm1_g, w_in, mu_shift)


def _head_sum(x, blk):
    h, lo = _split2(x)
    d = functools.partial(jnp.dot, preferred_element_type=F32)
    return d(h, blk) + d(lo, blk)


def _bdot(a, b):
    return jnp.dot(a, b, preferred_element_type=F32)


def _tri_inverse_batch(n_raw, ri, ci, before, eye):
    m1 = jnp.logical_and((ri >> 1) == (ci >> 1), before)
    t = [eye + jnp.where(m1, n, 0.0) for n in n_raw]
    sh = 1
    while (1 << sh) < CHUNK:
        mk = jnp.logical_and(jnp.logical_and((ri >> (sh + 1)) == (ci >> (sh + 1)),
                                             (ri >> sh) != (ci >> sh)), before)
        cm = [jnp.where(mk, n, 0.0).astype(BF16) for n in n_raw]
        tb = [x.astype(BF16) for x in t]
        xs = [_bdot(a, c).astype(BF16) for a, c in zip(tb, cm)]
        t = [x + _bdot(a, b) for x, a, b in zip(t, xs, tb)]
        sh += 1
    same = (ri >> 6) == (ci >> 6)
    strict = jnp.logical_and(same, before)
    out = []
    for n, t0 in zip(n_raw, t):
        nh, nl = _split2(jnp.where(strict, n, 0.0))
        th, tl = _split2(t0)
        p = _bdot(jnp.concatenate([nh, nl], axis=0), th)
        nt0 = (p[:2 * CHUNK] + p[2 * CHUNK:]) + _bdot(nh, tl)
        res = (eye - t0) + nt0
        out.append(t0 + _bdot(th, res.astype(BF16)))
    return out


def _scan_kernel(*refs, tbs, reverse, finalize):
    if finalize:
        (r_ref, k_ref, v_ref, lora_ref, w0_ref, w2_ref, a0_ref, a2_ref, kk_ref, ka_ref,
         rk_ref, g2_ref, lg_ref, lb_ref, a0o_ref, a2o_ref, yo_ref, y_ref, st_scr) = refs
    else:
        (r_ref, k_ref, v_ref, lora_ref, w0_ref, w2_ref, a0_ref, a2_ref, kk_ref, ka_ref, y_ref, st_scr) = refs

    @pl.when(pl.program_id(1) == 0)
    def _():
        st_scr[...] = jnp.zeros_like(st_scr)

    two = 2 * CHUNK
    ri = lax.broadcasted_iota(jnp.int32, (two, two), 0)
    ci = lax.broadcasted_iota(jnp.int32, (two, two), 1)
    same_head = (ri >> 6) == (ci >> 6)
    before = (ci > ri) if reverse else (ci < ri)
    eye = (ri == ci).astype(F32)
    bd_strict = jnp.logical_and(same_head, before)
    bd_incl = jnp.logical_and(same_head, jnp.logical_or(before, ri == ci))
    lane = lax.broadcasted_iota(jnp.int32, (CHUNK, LANES), 1)
    head0 = lane < HEAD_DIM
    rt_i = lax.broadcasted_iota(jnp.int32, (tbs, tbs), 0)
    ct_i = lax.broadcasted_iota(jnp.int32, (tbs, tbs), 1)
    tri = jnp.logical_and((rt_i >> 6) == (ct_i >> 6),
                          (ct_i >= rt_i) if reverse else (ct_i <= rt_i)).astype(F32)
    rb = lax.broadcasted_iota(jnp.int32, (D_RWKV, D_RWKV), 0)
    cb = lax.broadcasted_iota(jnp.int32, (D_RWKV, D_RWKV), 1)
    blk = ((rb >> 6) == (cb >> 6)).astype(BF16)

    def stack(x):
        return jnp.concatenate([jnp.where(head0, x, 0.0), jnp.where(head0, 0.0, x)], axis=0)

    k = k_ref[...]
    xa = lora_ref[:, LANES:2 * LANES]
    zw = w0_ref[...] + _dot(jnp.tanh(lora_ref[:, :LANES]), w2_ref[...])
    wd = -_softplus(-zw) - 0.5
    lw = -jnp.exp(wd)
    a = _sigmoid(a0_ref[...] + _dot(xa, a2_ref[...]))
    ka = ka_ref[...]
    kd = k * (1.0 + (a - 1.0) * ka)
    kkr = k * kk_ref[...]
    kkn = kkr * lax.rsqrt(jnp.maximum(_head_sum(kkr * kkr, blk), 1e-24))

    n_chunks = tbs // CHUNK
    units = [(ch, q) for ch in range(n_chunks) for q in range(N_PAIRS)]

    c_blk = _dot_exact_lhs(tri, lw)
    r_blk, v_blk = r_ref[...], v_ref[...]
    pre, n_raw = {}, []
    for ch, q in units:
        rows = slice(ch * CHUNK, (ch + 1) * CHUNK)
        ls = slice(q * LANES, (q + 1) * LANES)
        c, lwc = c_blk[rows, ls], lw[rows, ls]
        ctot = c[0:1, :] if reverse else c[CHUNK - 1:CHUNK, :]
        eg = jnp.exp(c)
        ei = jnp.exp(-c)
        ex = jnp.exp(c - lwc)
        ee = jnp.exp(ctot - c)
        akk = a[rows, ls] * kkn[rows, ls]
        kdc = kd[rows, ls]
        rt_s = stack(r_blk[rows, ls] * eg).astype(BF16)
        at_s = stack(-(kkn[rows, ls] * ex)).astype(BF16)
        bt = (akk * ei).astype(BF16)
        kt = (kdc * ei).astype(BF16)
        v_s = stack(v_blk[rows, ls]).astype(BF16)
        g = lax.dot_general(jnp.concatenate([at_s, rt_s], axis=0),
                            jnp.concatenate([bt, bt, kt, kt], axis=0),
                            (((1,), (1,)), ((), ())), preferred_element_type=F32)
        n_raw.append(g[:two, :two])
        pre[ch, q] = dict(
            rt_s=rt_s, at_s=at_s, v_s=v_s,
            aak=jnp.where(bd_strict, g[:two, two:], 0.0).astype(BF16),
            ark=jnp.where(bd_incl, g[two:, two:], 0.0).astype(BF16),
            arb=jnp.where(bd_incl, g[two:, :two], 0.0).astype(BF16),
            btet=jnp.transpose(stack(akk * ee)).astype(BF16),
            kvt=_bdot(jnp.transpose(stack(kdc * ee)).astype(BF16), v_s),
            gcol=jnp.transpose(jnp.broadcast_to(jnp.exp(ctot), (two, LANES))),
        )
    t_inv = _tri_inverse_batch(n_raw, ri, ci, before, eye)
    for (ch, q), t_u in zip(units, t_inv):
        d = pre[ch, q]
        avy = _bdot(jnp.concatenate([d["aak"], d["ark"]], axis=0), d["v_s"])
        d["yloc"] = avy[two:]
        th, tl = _split2(t_u)
        rhs = jnp.concatenate([d["at_s"], avy[:two].astype(BF16)], axis=1)
        wu = _bdot(jnp.concatenate([th, tl], axis=0), rhs)
        wu = wu[:two] + wu[two:]
        d["wm"] = wu[:, :LANES].astype(BF16)
        d["uloc"] = wu[:, LANES:]

    st = [st_scr[q] for q in range(N_PAIRS)]
    order = range(n_chunks - 1, -1, -1) if reverse else range(n_chunks)
    for ch in order:
        ys = []
        for q in range(N_PAIRS):
            d = pre[ch, q]
            p = _bdot(jnp.concatenate([d["wm"], d["rt_s"]], axis=0), st[q].astype(BF16))
            u = (p[:two] + d["uloc"]).astype(BF16)
            y_s = p[two:] + _bdot(d["arb"], u) + d["yloc"]
            ys.append(y_s[:CHUNK] + y_s[CHUNK:])
            st[q] = st[q] * d["gcol"] + _bdot(d["btet"], u) + d["kvt"]
        y_ref[ch * CHUNK:(ch + 1) * CHUNK, :] = jnp.concatenate(ys, axis=1)
    for q in range(N_PAIRS):
        st_scr[q] = st[q]

    if finalize:
        y = y_ref[...] + yo_ref[...]
        inv_n = 1.0 / HEAD_DIM
        mean = _head_sum(y, blk) * inv_n
        d = y - mean
        var = _head_sum(d * d, blk) * inv_n
        yn = d * lax.rsqrt(var + LNX_EPS) * lg_ref[...] + lb_ref[...]
        a_o = _sigmoid(a0o_ref[...] + _dot(xa, a2o_ref[...]))
        kd_o = k * (1.0 + (a_o - 1.0) * ka)
        rr = r_ref[...] * rk_ref[...]
        bonus = (_head_sum(rr * kd, blk) + _head_sum(rr * kd_o, blk)) * v_ref[...]
        gate = _dot(_sigmoid(lora_ref[:, 2 * LANES:]), g2_ref[...])
        y_ref[...] = (yn + bonus) * gate


def _scan_call(ps, vecs, mats, y_other, *, direction, tbs):
    bsz, t_len, _ = ps.shape
    nt = t_len // tbs
    reverse = direction == 1
    finalize = y_other is not None
    lora_w = RWKV_COLS - 3 * D_RWKV

    def tmap(i):
        return (nt - 1 - i) if reverse else i

    def col_spec(col):
        return pl.BlockSpec((None, tbs, D_RWKV), lambda b, i: (b, tmap(i), col))

    vec_spec = _const_spec((1, D_RWKV))
    mat_spec = _const_spec((LANES, D_RWKV))
    in_specs = [
        col_spec(0), col_spec(1), col_spec(2),
        pl.BlockSpec((None, tbs, lora_w), lambda b, i: (b, tmap(i), 3 * D_RWKV // lora_w)),
        vec_spec, mat_spec, vec_spec, mat_spec, vec_spec, vec_spec,
    ]
    o = 1 - direction
    args = [ps] * 4 + [vecs["w0"][direction], mats["w2p"][direction], vecs["a0"][direction],
                       mats["a2p"][direction], vecs["k_k"], vecs["k_a"]]
    if finalize:
        in_specs += [vec_spec, mat_spec, vec_spec, vec_spec, vec_spec, mat_spec, col_spec(0)]
        args += [vecs["r_k"], mats["g2"], vecs["lnx_g"], vecs["lnx_b"], vecs["a0"][o], mats["a2p"][o], y_other]
    kern = functools.partial(_scan_kernel, tbs=tbs, reverse=reverse, finalize=finalize)
    scratch = [pltpu.VMEM((N_PAIRS, 2 * CHUNK, LANES), F32)]
    return pl.pallas_call(
        kern,
        grid=(bsz, nt),
        in_specs=in_specs,
        out_specs=col_spec(0),
        out_shape=jax.ShapeDtypeStruct((bsz, t_len, D_RWKV), F32),
        scratch_shapes=scratch,
        compiler_params=pltpu.CompilerParams(
            dimension_semantics=("arbitrary", "arbitrary"), vmem_limit_bytes=VMEM_LIMIT),
        name="rwkv7_scan_bwd" if reverse else "rwkv7_scan_fwd",
    )(*args)


S5_PAIR_LANES = 2 * S5_GROUP
S5_PAIRS_PER_STEP = LANES // S5_PAIR_LANES
S5_WIDTH = S5_CHUNK * S5_PAIR_LANES
S5_TOK_PER_VREG = LANES // S5_PAIR_LANES


def _gelu_tanh(x):
    return 0.5 * x * (1.0 + jnp.tanh(math.sqrt(2.0 / math.pi) * (x + 0.044715 * (x * x * x))))


def _s5_pair(u, wz, tz, cz, pw_ref, pj, dvec, wg, bg, nc, nlev):
    ub = u.astype(BF16)
    z = jnp.dot(ub, wz, preferred_element_type=F32)
    row = lax.broadcasted_iota(jnp.int32, (nc, LANES), 0)
    xs = [z[:, j * LANES:(j + 1) * LANES] for j in range(4)]

    def shifted(x, s, up):
        if up:
            return jnp.where(row >= nc - s, 0.0, pltpu.roll(x, nc - s, 0))
        return jnp.where(row < s, 0.0, pltpu.roll(x, s, 0))

    for lev in range(nlev):
        s = 1 << lev
        for d in range(2):
            xr, xi = xs[2 * d], xs[2 * d + 1]
            ar = pw_ref[pj, lev, 2 * d:2 * d + 1, :]
            ai = pw_ref[pj, lev, 2 * d + 1:2 * d + 2, :]
            pr = shifted(xr, s, d == 1)
            pi = shifted(xi, s, d == 1)
            xs[2 * d] = xr + (ar * pr - ai * pi)
            xs[2 * d + 1] = xi + (ar * pi + ai * pr)
    xin = jnp.concatenate([shifted(xs[0], 1, False), shifted(xs[1], 1, False),
                           shifted(xs[2], 1, True), shifted(xs[3], 1, True)], axis=1)
    y = (jnp.dot(ub, tz, preferred_element_type=F32)
         + jnp.dot(xin.astype(BF16), cz, preferred_element_type=F32)
         + dvec * u)
    zact = _gelu_tanh(y)
    gate = _sigmoid(jnp.dot(zact.astype(BF16), wg, preferred_element_type=F32) + bg)
    return zact * gate


def _s5_kernel(u_ref, wz_ref, tz_ref, cz_ref, pw_ref, d_ref, wg_ref, bg_ref, o_ref, *, nc, nlev):
    lane = lax.broadcasted_iota(jnp.int32, (nc, LANES), 1)
    seg = lane // S5_PAIR_LANES
    tok = [u_ref[pl.ds(s, nc, stride=S5_CHUNK), :] for s in range(S5_CHUNK)]
    outs = []
    for pj in range(S5_PAIRS_PER_STEP):
        cols = []
        for qv in range(S5_CHUNK // S5_TOK_PER_VREG):
            acc = None
            for i in range(S5_TOK_PER_VREG):
                x = tok[S5_TOK_PER_VREG * qv + i]
                sh = (S5_PAIR_LANES * (i - pj)) % LANES
                if sh:
                    x = pltpu.roll(x, sh, 1)
                acc = x if acc is None else jnp.where(seg == i, x, acc)
            cols.append(acc)
        u = jnp.concatenate(cols, axis=1)
        outs.append(_s5_pair(u, wz_ref[pj], tz_ref[pj], cz_ref[pj], pw_ref, pj, d_ref[pj], wg_ref[pj],
                             bg_ref[pj], nc, nlev))
    for s in range(S5_CHUNK):
        qv, i = divmod(s, S5_TOK_PER_VREG)
        acc = None
        for pj in range(S5_PAIRS_PER_STEP):
            x = outs[pj][:, qv * LANES:(qv + 1) * LANES]
            sh = (S5_PAIR_LANES * (pj - i)) % LANES
            if sh:
                x = pltpu.roll(x, sh, 1)
            acc = x if acc is None else jnp.where(seg == pj, x, acc)
        o_ref[pl.ds(s, nc, stride=S5_CHUNK), :] = acc


def _s5_call(u, tabs):
    bsz, t_len, width = u.shape
    nc = t_len // S5_CHUNK
    nsteps = width // LANES
    nlev = max(1, (nc - 1).bit_length())
    kern = functools.partial(_s5_kernel, nc=nc, nlev=nlev)
    pp = S5_PAIRS_PER_STEP

    def tab_spec(arr):
        nd = arr.ndim - 1
        return pl.BlockSpec((pp,) + arr.shape[1:], lambda j, b: (j,) + (0,) * nd)

    names = ["wz", "tz", "cz", "pw", "dvec", "wglu", "bglu"]
    return pl.pallas_call(
        kern,
        grid=(nsteps, bsz),
        in_specs=[pl.BlockSpec((None, t_len, LANES), lambda j, b: (b, 0, j))] + [tab_spec(tabs[n]) for n in names],
        out_specs=pl.BlockSpec((None, t_len, LANES), lambda j, b: (b, 0, j)),
        out_shape=jax.ShapeDtypeStruct(u.shape, F32),
        compiler_params=pltpu.CompilerParams(vmem_limit_bytes=VMEM_LIMIT),
        name="s5_mixer",
    )(u, *[tabs[n] for n in names])


def _s5_tables(lam_re, lam_im, log_dt, b_re, b_im, c_re, c_im, d_skip, w_glu, b_glu, max_levels):
    hp = lax.Precision.HIGHEST
    L, G, P, H = S5_CHUNK, N_S5_GROUPS, S5_STATE, S5_GROUP
    dt = jnp.exp(log_dt)[:, :, None]
    are, aim = lam_re * dt, lam_im * dt

    def apow(n):
        n = jnp.asarray(n, F32)
        shape = n.shape + (1, 1, 1)
        mag = jnp.exp(n.reshape(shape) * are)
        ang = n.reshape(shape) * aim
        return mag * jnp.cos(ang), mag * jnp.sin(ang)

    a1r, a1i = apow(jnp.ones(()))
    den = lam_re * lam_re + lam_im * lam_im
    qr = ((a1r - 1.0) * lam_re + a1i * lam_im) / den
    qi = (a1i * lam_re - (a1r - 1.0) * lam_im) / den
    bbr = qr[..., None] * b_re - qi[..., None] * b_im
    bbi = qr[..., None] * b_im + qi[..., None] * b_re
    pr, pi = apow(jnp.arange(L + 1))

    abr = pr[..., None] * bbr - pi[..., None] * bbi
    abi = pr[..., None] * bbi + pi[..., None] * bbr
    s_idx = jnp.arange(L)
    wf_r = jnp.transpose(abr[L - 1 - s_idx, 0], (1, 0, 3, 2))
    wf_i = jnp.transpose(abi[L - 1 - s_idx, 0], (1, 0, 3, 2))
    wb_r = jnp.transpose(abr[s_idx, 1], (1, 0, 3, 2))
    wb_i = jnp.transpose(abi[s_idx, 1], (1, 0, 3, 2))
    kf = (jnp.einsum("ghp,ngpi->nghi", c_re[0], abr[:L, 0], precision=hp)
          - jnp.einsum("ghp,ngpi->nghi", c_im[0], abi[:L, 0], precision=hp))
    kb = (jnp.einsum("ghp,ngpi->nghi", c_re[1], abr[:L, 1], precision=hp)
          - jnp.einsum("ghp,ngpi->nghi", c_im[1], abi[:L, 1], precision=hp))
    lag = s_idx[None, :] - s_idx[:, None]
    tf = jnp.where((lag >= 0)[:, :, None, None, None], kf[jnp.clip(lag, 0, L - 1)], 0.0)
    tb = jnp.where((lag <= 0)[:, :, None, None, None], kb[jnp.clip(-lag, 0, L - 1)], 0.0)
    tz_g = jnp.transpose(tf + tb, (2, 0, 4, 1, 3))
    t1 = s_idx + 1
    cf_r = (c_re[0][:, None] * pr[t1, 0].transpose(1, 0, 2)[:, :, None, :]
            - c_im[0][:, None] * pi[t1, 0].transpose(1, 0, 2)[:, :, None, :])
    cf_i = (c_re[0][:, None] * pi[t1, 0].transpose(1, 0, 2)[:, :, None, :]
            + c_im[0][:, None] * pr[t1, 0].transpose(1, 0, 2)[:, :, None, :])
    t2 = L - s_idx
    cb_r = (c_re[1][:, None] * pr[t2, 1].transpose(1, 0, 2)[:, :, None, :]
            - c_im[1][:, None] * pi[t2, 1].transpose(1, 0, 2)[:, :, None, :])
    cb_i = (c_re[1][:, None] * pi[t2, 1].transpose(1, 0, 2)[:, :, None, :]
            + c_im[1][:, None] * pr[t2, 1].transpose(1, 0, 2)[:, :, None, :])

    eye2 = jnp.eye(2, dtype=F32)
    g2 = G // 2

    def pair_sq(m):
        m = m.reshape(g2, 2, L, H, L, H)
        m = m[:, :, :, :, :, None, :] * eye2[None, :, None, None, None, :, None]
        return jnp.transpose(m, (0, 2, 1, 3, 4, 5, 6)).reshape(g2, L * 2 * H, L * 2 * H)

    def pair_in(m):
        m = m.reshape(g2, 2, L, H, P)
        m = m[:, :, :, :, None, :] * eye2[None, :, None, None, :, None]
        return jnp.transpose(m, (0, 2, 1, 3, 4, 5)).reshape(g2, L * 2 * H, 2 * P)

    def pair_out(m):
        m = jnp.transpose(m, (0, 3, 1, 2)).reshape(g2, 2, P, L, H)
        m = m[:, :, :, :, None, :] * eye2[None, :, None, None, :, None]
        return m.reshape(g2, 2 * P, L * 2 * H)

    wz = jnp.concatenate([pair_in(m) for m in (wf_r, wf_i, wb_r, wb_i)], axis=2)
    cz = jnp.concatenate([pair_out(m) for m in (cf_r, -cf_i, cb_r, -cb_i)], axis=1)
    tz = pair_sq(tz_g)
    eye_l = jnp.eye(L, dtype=F32)
    wglu = pair_sq(eye_l[None, :, None, :, None] * w_glu[:, None, :, None, :])

    def lanes_gh(vec):
        return jnp.broadcast_to(vec.reshape(g2, 1, 2, H), (g2, L, 2, H)).reshape(g2, 1, L * 2 * H)

    lev_n = (L * (2 ** jnp.arange(max_levels))).astype(F32)
    lr, li = apow(lev_n)

    def lanes_gp(m):
        return jnp.transpose(m.reshape(max_levels, G // 2, 2 * P), (1, 0, 2))[:, :, None, :]

    pw = jnp.concatenate([lanes_gp(lr[:, 0]), lanes_gp(li[:, 0]), lanes_gp(lr[:, 1]), lanes_gp(li[:, 1])],
                         axis=2)
    return {
        "wz": wz.astype(BF16), "tz": tz.astype(BF16), "cz": cz.astype(BF16), "pw": pw,
        "dvec": lanes_gh(d_skip), "wglu": wglu.astype(BF16), "bglu": lanes_gh(b_glu),
    }


def _rms(x):
    return x * lax.rsqrt(jnp.mean(x * x, axis=-1, keepdims=True) + RMS_EPS)


def _out_kernel(x_ref, yr_ref, zs_ref, mod_ref, sg_ref, wo_ref, n2_ref, w1_ref, w3_ref, w2_ref, fg_ref, o_ref):
    gate1 = mod_ref[2:3, :]
    shift2 = mod_ref[3:4, :]
    scale2 = mod_ref[4:5, :]
    gate2 = mod_ref[5:6, :]
    ys = _rms(zs_ref[...]) * sg_ref[...]
    mix = (jnp.dot(yr_ref[...].astype(BF16), wo_ref[:D_RWKV, :], preferred_element_type=F32)
           + jnp.dot(ys.astype(BF16), wo_ref[D_RWKV:, :], preferred_element_type=F32))
    x1 = x_ref[...] + gate1 * mix
    h = ((_rms(x1) * n2_ref[...]) * (1.0 + scale2) + shift2).astype(BF16)
    f1 = jnp.dot(h, w1_ref[...], preferred_element_type=F32)
    f3 = jnp.dot(h, w3_ref[...], preferred_element_type=F32)
    act = (f1 * _sigmoid(f1)) * f3
    f = jnp.dot(act.astype(BF16), w2_ref[...], preferred_element_type=F32)
    x2 = x1 + gate2 * f
    o_ref[...] = _rms(x2) * fg_ref[...]


def _out_call(x, y_rwkv, z_s5, mod3, s5_out_g, w_out, norm2_g, w_ff1, w_ff3, w_ff2, final_g, tb):
    bsz, t_len, _ = x.shape

    def resident(shape):
        return pl.BlockSpec(shape, lambda b, i: (0, 0), pipeline_mode=pl.Buffered(1))

    return pl.pallas_call(
        _out_kernel,
        grid=(bsz, t_len // tb),
        in_specs=[
            pl.BlockSpec((None, tb, D_MODEL), lambda b, i: (b, i, 0)),
            pl.BlockSpec((None, tb, D_RWKV), lambda b, i: (b, i, 0)),
            pl.BlockSpec((None, tb, D_S5), lambda b, i: (b, i, 0)),
            pl.BlockSpec((None, 6, D_MODEL), lambda b, i: (b, 0, 0)),
            _const_spec((1, D_S5)),
            resident((D_MODEL, D_MODEL)),
            _const_spec((1, D_MODEL)),
            resident((D_MODEL, D_FF)),
            resident((D_MODEL, D_FF)),
            resident((D_FF, D_MODEL)),
            _const_spec((1, D_MODEL)),
        ],
        out_specs=pl.BlockSpec((None, tb, D_MODEL), lambda b, i: (b, i, 0)),
        out_shape=jax.ShapeDtypeStruct(x.shape, F32),
        compiler_params=pltpu.CompilerParams(vmem_limit_bytes=VMEM_LIMIT),
        name="out_proj_ffn",
    )(x, y_rwkv, z_s5, mod3, s5_out_g, w_out, norm2_g, w_ff1, w_ff3, w_ff2, final_g)


def _block_sizes(t_len):
    tb_in = min(512, t_len)
    tbs = min(256, t_len)
    tb_out = min(256, t_len)
    return tb_in, tbs, tb_out


def _pad_lora(w, n_in):
    z = jnp.zeros_like(w[0])
    return [jnp.concatenate([w[0], z], axis=0).astype(BF16), jnp.concatenate([z, w[1]], axis=0).astype(BF16)]


def _trunk(x, mod3, prm, s5_tabs):
    bsz, t_len, _ = x.shape
    tb_in, tbs, tb_out = _block_sizes(t_len)
    ps, u = _in_call(x, mod3, prm["norm1_g"], prm["w_in"], prm["mu_shift"], tb_in)
    y_b = _scan_call(ps, prm["vecs"], prm["mats"], None, direction=1, tbs=tbs)
    y_rwkv = _scan_call(ps, prm["vecs"], prm["mats"], y_b, direction=0, tbs=tbs)
    z_s5 = _s5_call(u, s5_tabs)
    return _out_call(x, y_rwkv, z_s5, mod3, prm["s5_out_g"], prm["w_out"], prm["norm2_g"],
                     prm["w_ff1"], prm["w_ff3"], prm["w_ff2"], prm["final_g"], tb_out)


def _prepare(norm1_g, w_in, mu_shift, w0, w2, a0, a2, g2, k_k, k_a, r_k, lnx_g, lnx_b, s5_out_g, w_out,
             norm2_g, w_ff1, w_ff3, w_ff2, final_g):
    row = lambda v: v.reshape(1, -1)
    vecs = {
        "w0": [row(w0[0]), row(w0[1])], "a0": [row(a0[0]), row(a0[1])],
        "k_k": row(k_k), "k_a": row(k_a), "r_k": row(r_k), "lnx_g": row(lnx_g), "lnx_b": row(lnx_b),
    }
    mats = {"w2p": _pad_lora(w2, 64), "a2p": _pad_lora(a2, 64), "g2": g2.astype(BF16)}
    return {
        "norm1_g": row(norm1_g), "w_in": w_in.astype(BF16), "mu_shift": row(mu_shift),
        "vecs": vecs, "mats": mats, "s5_out_g": row(s5_out_g), "w_out": w_out.astype(BF16),
        "norm2_g": row(norm2_g), "w_ff1": w_ff1.astype(BF16), "w_ff3": w_ff3.astype(BF16),
        "w_ff2": w_ff2.astype(BF16), "final_g": row(final_g),
    }


def kernel(x_prompt, x_sample, c_prompt, c_sample, norm1_g, w_ada, b_ada, w_in, mu_shift, w0, w2, a0, a2, g2,
           k_k, k_a, r_k, lnx_g, lnx_b, lam_re, lam_im, log_dt, b_re, b_im, c_re, c_im, d_skip, w_glu, b_glu,
           s5_out_g, w_out, norm2_g, w_ff1, w_ff3, w_ff2, final_g):
    depth = norm1_g.shape[0]
    assert depth == 1, "the fused output kernel applies the final norm, so it supports a single layer"
    nbp = x_prompt.shape[0]
    xs = [x_prompt, x_sample]
    c_all = jnp.concatenate([c_prompt, c_sample], axis=0)
    max_t = max(x_prompt.shape[1], x_sample.shape[1])
    max_levels = max(1, (max_t // S5_CHUNK - 1).bit_length())
    for i in range(depth):
        prm = _prepare(norm1_g[i], w_in[i], mu_shift[i], w0[i], w2[i], a0[i], a2[i], g2[i], k_k[i], k_a[i],
                       r_k[i], lnx_g[i], lnx_b[i], s5_out_g[i], w_out[i], norm2_g[i], w_ff1[i], w_ff3[i],
                       w_ff2[i], final_g)
        s5_tabs = _s5_tables(lam_re[i], lam_im[i], log_dt[i], b_re[i], b_im[i], c_re[i], c_im[i],
                             d_skip[i], w_glu[i], b_glu[i], max_levels)
        mod = _mod_call(c_all, w_ada[i].astype(BF16), b_ada[i].reshape(1, -1))
        mod3 = mod.reshape(c_all.shape[0], 6, D_MODEL)
        xs = [_trunk(xs[0], mod3[:nbp], prm, s5_tabs), _trunk(xs[1], mod3[nbp:], prm, s5_tabs)]
    return tuple(xs)
```

```python
import functools
import math

import jax
import jax.numpy as jnp
from jax import lax
from jax.experimental import pallas as pl
from jax.experimental.pallas import tpu as pltpu

F32 = jnp.float32
BF16 = jnp.bfloat16

D_MODEL = 1024
D_RWKV = 512
HEAD_DIM = 64
D_S5 = 512
S5_GROUP = 16
N_S5_GROUPS = 32
S5_STATE = 64
D_FF = 2816
RWKV_COLS = 1920
D_IN_PROJ = 2432
RMS_EPS = 1e-6
LNX_EPS = 64e-5

LANES = 128
CHUNK = 64
S5_CHUNK = 16
N_PAIRS = D_RWKV // LANES
N_S5_PAIRS = N_S5_GROUPS // 2
VMEM_LIMIT = 56 * 1024 * 1024


def _dot(a, b):
    return jnp.dot(a.astype(BF16), b.astype(BF16), preferred_element_type=F32)


def _split2(x):
    hi = x.astype(BF16)
    lo = (x - hi.astype(F32)).astype(BF16)
    return hi, lo


def _split3(x):
    hi = x.astype(BF16)
    r1 = x - hi.astype(F32)
    mid = r1.astype(BF16)
    lo = (r1 - mid.astype(F32)).astype(BF16)
    return hi, mid, lo


def _dot_exact_lhs(m, x):
    mh = m.astype(BF16)
    h, mid, lo = _split3(x)
    d = functools.partial(jnp.dot, preferred_element_type=F32)
    return d(mh, h) + (d(mh, mid) + d(mh, lo))


def _sigmoid(x):
    return 1.0 / (1.0 + jnp.exp(-x))


def _softplus(x):
    return jnp.maximum(x, 0.0) + jnp.log1p(jnp.exp(-jnp.abs(x)))


def _const_spec(shape):
    nd = len(shape)
    return pl.BlockSpec(shape, lambda *_: (0,) * nd)


def _mod_kernel(c_ref, w_ref, b_ref, o_ref):
    c = c_ref[...]
    s = c * _sigmoid(c)
    o_ref[...] = _dot(s, w_ref[...]) + b_ref[...]


def _mod_call(c_all, w_ada, b_ada):
    nb = c_all.shape[0]
    ncol = w_ada.shape[1] // D_MODEL
    return pl.pallas_call(
        _mod_kernel,
        grid=(ncol,),
        in_specs=[
            pl.BlockSpec((nb, D_MODEL), lambda j: (0, 0)),
            pl.BlockSpec((D_MODEL, D_MODEL), lambda j: (0, j)),
            pl.BlockSpec((1, D_MODEL), lambda j: (0, j)),
        ],
        out_specs=pl.BlockSpec((nb, D_MODEL), lambda j: (0, j)),
        out_shape=jax.ShapeDtypeStruct((nb, w_ada.shape[1]), F32),
        name="adaln_mod",
    )(c_all, w_ada, b_ada)


HALO = 8


def _in_kernel(x_ref, xp_ref, xn_ref, mod_ref, g_ref, w_ref, mu_ref, ps_ref, u_ref, p_scr, *, tb, nt):
    i = pl.program_id(1)
    shift = mod_ref[0:1, :]
    scale = mod_ref[1:2, :]
    g = g_ref[...]
    xa = jnp.concatenate([xp_ref[...], x_ref[...], xn_ref[...]], axis=0)
    ms = jnp.mean(xa * xa, axis=-1, keepdims=True)
    h = ((xa * lax.rsqrt(ms + RMS_EPS)) * g) * (1.0 + scale) + shift
    hb = h.astype(BF16)
    p_scr[...] = jnp.dot(hb, w_ref[:, :RWKV_COLS], preferred_element_type=F32)
    u_ref[...] = jnp.dot(hb, w_ref[:, RWKV_COLS:], preferred_element_type=F32)[HALO:HALO + tb]
    p = p_scr[HALO:HALO + tb, :]
    prev = p_scr[HALO - 1:HALO - 1 + tb, :]
    nxt = p_scr[HALO + 1:HALO + 1 + tb, :]
    row = lax.broadcasted_iota(jnp.int32, (tb, 1), 0)
    prev = jnp.where(jnp.logical_and(row == 0, i == 0), 0.0, prev)
    nxt = jnp.where(jnp.logical_and(row == tb - 1, i == nt - 1), 0.0, nxt)
    ps_ref[...] = p + mu_ref[...] * (0.5 * (prev + nxt) - p)


def _in_call(x, mod3, norm1_g, w_in, mu_shift, tb):
    bsz, t_len, _ = x.shape
    nt = t_len // tb
    hb = tb // HALO
    last = t_len // HALO - 1
    kern = functools.partial(_in_kernel, tb=tb, nt=nt)
    return pl.pallas_call(
        kern,
        grid=(bsz, nt),
        in_specs=[
            pl.BlockSpec((None, tb, D_MODEL), lambda b, i: (b, i, 0)),
            pl.BlockSpec((None, HALO, D_MODEL), lambda b, i: (b, jnp.maximum(i * hb - 1, 0), 0)),
            pl.BlockSpec((None, HALO, D_MODEL), lambda b, i: (b, jnp.minimum((i + 1) * hb, last), 0)),
            pl.BlockSpec((None, 6, D_MODEL), lambda b, i: (b, 0, 0)),
            _const_spec((1, D_MODEL)),
            pl.BlockSpec((D_MODEL, D_IN_PROJ), lambda b, i: (0, 0), pipeline_mode=pl.Buffered(1)),
            _const_spec((1, RWKV_COLS)),
        ],
        out_specs=[
            pl.BlockSpec((None, tb, RWKV_COLS), lambda b, i: (b, i, 0)),
            pl.BlockSpec((None, tb, D_S5), lambda b, i: (b, i, 0)),
        ],
        out_shape=[
            jax.ShapeDtypeStruct((bsz, t_len, RWKV_COLS), F32),
            jax.ShapeDtypeStruct((bsz, t_len, D_S5), F32),
        ],
        scratch_shapes=[pltpu.VMEM((tb + 2 * HALO, RWKV_COLS), F32)],
        compiler_params=pltpu.CompilerParams(vmem_limit_bytes=VMEM_LIMIT),
        name="in_proj_shift",
    )(x, x, x, mod3, norm1_g, w_in, mu_shift)


def _head_sum(x, blk):
    h, lo = _split2(x)
    d = functools.partial(jnp.dot, preferred_element_type=F32)
    return d(h, blk) + d(lo, blk)


def _bdot(a, b):
    return jnp.dot(a, b, preferred_element_type=F32)


def _side_by_side_diag(x):
    z = jnp.zeros((x.shape[0], LANES), x.dtype)
    return jnp.concatenate([jnp.concatenate([x[:, :LANES], z], axis=1),
                            jnp.concatenate([z, x[:, LANES:]], axis=1)], axis=0)


def _tri_inverse_batch(n_raw, reverse):
    two = 2 * CHUNK
    ri = lax.broadcasted_iota(jnp.int32, (two, 2 * LANES), 0)
    ci = lax.broadcasted_iota(jnp.int32, (two, 2 * LANES), 1) & (LANES - 1)
    before = (ci > ri) if reverse else (ci < ri)
    eye = (ri == ci).astype(F32)
    n_sbs = [jnp.concatenate([n_raw[i], n_raw[i + 1]], axis=1) for i in range(0, len(n_raw), 2)]
    m1 = jnp.logical_and((ri >> 1) == (ci >> 1), before)
    t = [eye + jnp.where(m1, n, 0.0) for n in n_sbs]
    sh = 1
    while (1 << sh) < CHUNK:
        mk = jnp.logical_and(jnp.logical_and((ri >> (sh + 1)) == (ci >> (sh + 1)),
                                             (ri >> sh) != (ci >> sh)), before)
        cm = [_side_by_side_diag(jnp.where(mk, n, 0.0).astype(BF16)) for n in n_sbs]
        tb = [x.astype(BF16) for x in t]
        xs = [_bdot(a, c).astype(BF16) for a, c in zip(tb, cm)]
        t = [x + _bdot(a, _side_by_side_diag(b)) for x, a, b in zip(t, xs, tb)]
        sh += 1
    out = []
    for t0 in t:
        out += [t0[:, :LANES], t0[:, LANES:]]
    return out


def _scan_kernel(*refs, tbs, reverse, finalize):
    if finalize:
        (r_ref, k_ref, v_ref, lora_ref, w0_ref, w2_ref, a0_ref, a2_ref, kk_ref, ka_ref,
         rk_ref, g2_ref, lg_ref, lb_ref, a0o_ref, a2o_ref, yo_ref, y_ref, st_scr) = refs
    else:
        (r_ref, k_ref, v_ref, lora_ref, w0_ref, w2_ref, a0_ref, a2_ref, kk_ref, ka_ref, y_ref, st_scr) = refs

    @pl.when(pl.program_id(1) == 0)
    def _():
        st_scr[...] = jnp.zeros_like(st_scr)

    two = 2 * CHUNK
    ri = lax.broadcasted_iota(jnp.int32, (two, two), 0)
    ci = lax.broadcasted_iota(jnp.int32, (two, two), 1)
    same_head = (ri >> 6) == (ci >> 6)
    before = (ci > ri) if reverse else (ci < ri)
    bd_strict = jnp.logical_and(same_head, before)
    bd_incl = jnp.logical_and(same_head, jnp.logical_or(before, ri == ci))
    lane = lax.broadcasted_iota(jnp.int32, (CHUNK, LANES), 1)
    head0 = lane < HEAD_DIM
    rt_i = lax.broadcasted_iota(jnp.int32, (tbs, tbs), 0)
    ct_i = lax.broadcasted_iota(jnp.int32, (tbs, tbs), 1)
    tri = jnp.logical_and((rt_i >> 6) == (ct_i >> 6),
                          (ct_i >= rt_i) if reverse else (ct_i <= rt_i)).astype(F32)
    rb = lax.broadcasted_iota(jnp.int32, (D_RWKV, D_RWKV), 0)
    cb = lax.broadcasted_iota(jnp.int32, (D_RWKV, D_RWKV), 1)
    blk = ((rb >> 6) == (cb >> 6)).astype(BF16)

    def stack(x):
        return jnp.concatenate([jnp.where(head0, x, 0.0), jnp.where(head0, 0.0, x)], axis=0)

    k = k_ref[...]
    xa = lora_ref[:, LANES:2 * LANES]
    zw = w0_ref[...] + _dot(jnp.tanh(lora_ref[:, :LANES]), w2_ref[...])
    wd = -_softplus(-zw) - 0.5
    lw = -jnp.exp(wd)
    a = _sigmoid(a0_ref[...] + _dot(xa, a2_ref[...]))
    ka = ka_ref[...]
    kd = k * (1.0 + (a - 1.0) * ka)
    kkr = k * kk_ref[...]
    kkn = kkr * lax.rsqrt(jnp.maximum(_head_sum(kkr * kkr, blk), 1e-24))

    n_chunks = tbs // CHUNK
    units = [(ch, q) for ch in range(n_chunks) for q in range(N_PAIRS)]

    c_blk = _dot_exact_lhs(tri, lw)
    r_blk, v_blk = r_ref[...], v_ref[...]
    pre, n_raw = {}, []
    for ch, q in units:
        rows = slice(ch * CHUNK, (ch + 1) * CHUNK)
        ls = slice(q * LANES, (q + 1) * LANES)
        c, lwc = c_blk[rows, ls], lw[rows, ls]
        ctot = c[0:1, :] if reverse else c[CHUNK - 1:CHUNK, :]
        eg = jnp.exp(c)
        ei = jnp.exp(-c)
        ex = jnp.exp(c - lwc)
        ee = jnp.exp(ctot - c)
        akk = a[rows, ls] * kkn[rows, ls]
        kdc = kd[rows, ls]
        rt = r_blk[rows, ls] * eg
        at = -(kkn[rows, ls] * ex)
        g = lax.dot_general(jnp.concatenate([at, rt], axis=0).astype(BF16),
                            jnp.concatenate([stack(akk * ei), stack(kdc * ei)], axis=0).astype(BF16),
                            (((1,), (1,)), ((), ())), preferred_element_type=F32)
        n_raw.append(stack(g[:CHUNK, :two]))
        pre[ch, q] = dict(
            rt_s=stack(rt).astype(BF16), at_s=stack(at).astype(BF16), v_s=stack(v_blk[rows, ls]).astype(BF16),
            aak=jnp.where(bd_strict, stack(g[:CHUNK, two:]), 0.0).astype(BF16),
            ark=jnp.where(bd_incl, stack(g[CHUNK:, two:]), 0.0).astype(BF16),
            arb=jnp.where(bd_incl, stack(g[CHUNK:, :two]), 0.0).astype(BF16),
            btet=jnp.transpose(stack(akk * ee)).astype(BF16),
            ktet=jnp.transpose(stack(kdc * ee)).astype(BF16),
            gcol=jnp.transpose(jnp.broadcast_to(jnp.exp(ctot), (two, LANES))),
        )
    t_inv = _tri_inverse_batch(n_raw, reverse)

    def sbs(ch, m, name):
        return jnp.concatenate([pre[ch, 2 * m][name], pre[ch, 2 * m + 1][name]], axis=1)

    half = N_PAIRS // 2
    duo = {}
    for ch in range(n_chunks):
        for m in range(half):
            lhs = jnp.concatenate([sbs(ch, m, "aak"), sbs(ch, m, "ark"), sbs(ch, m, "ktet")], axis=0)
            avk = _bdot(lhs, _side_by_side_diag(sbs(ch, m, "v_s")))
            wms, ulocs = [], []
            for j in range(2):
                q = 2 * m + j
                av = avk[:two, j * LANES:(j + 1) * LANES].astype(BF16)
                rhs = jnp.concatenate([pre[ch, q]["at_s"], av], axis=1)
                wu = _bdot(t_inv[ch * N_PAIRS + q].astype(BF16), rhs)
                wms.append(wu[:, :LANES].astype(BF16))
                ulocs.append(wu[:, LANES:])
            duo[ch, m] = dict(
                lhs_p=jnp.concatenate([jnp.concatenate(wms, axis=1), sbs(ch, m, "rt_s")], axis=0),
                lhs_u=jnp.concatenate([sbs(ch, m, "arb"), sbs(ch, m, "btet")], axis=0),
                uloc=jnp.concatenate(ulocs, axis=1), yloc=avk[two:2 * two], kvt=avk[2 * two:],
                gcol=sbs(ch, m, "gcol"),
            )

    st = [st_scr[m] for m in range(half)]
    order = range(n_chunks - 1, -1, -1) if reverse else range(n_chunks)
    for ch in order:
        ys = []
        for m in range(half):
            d = duo[ch, m]
            p = _bdot(d["lhs_p"], _side_by_side_diag(st[m].astype(BF16)))
            u = _side_by_side_diag((p[:two] + d["uloc"]).astype(BF16))
            ab = _bdot(d["lhs_u"], u)
            y_s = p[two:] + ab[:two] + d["yloc"]
            ys.append(y_s[:CHUNK] + y_s[CHUNK:])
            st[m] = st[m] * d["gcol"] + ab[two:] + d["kvt"]
        y_ref[ch * CHUNK:(ch + 1) * CHUNK, :] = jnp.concatenate(ys, axis=1)
    for m in range(half):
        st_scr[m] = st[m]

    if finalize:
        y = y_ref[...] + yo_ref[...]
        inv_n = 1.0 / HEAD_DIM
        mean = _head_sum(y, blk) * inv_n
        d = y - mean
        var = _head_sum(d * d, blk) * inv_n
        yn = d * lax.rsqrt(var + LNX_EPS) * lg_ref[...] + lb_ref[...]
        a_o = _sigmoid(a0o_ref[...] + _dot(xa, a2o_ref[...]))
        kd_o = k * (1.0 + (a_o - 1.0) * ka)
        rr = r_ref[...] * rk_ref[...]
        bonus = (_head_sum(rr * kd, blk) + _head_sum(rr * kd_o, blk)) * v_ref[...]
        gate = _dot(_sigmoid(lora_ref[:, 2 * LANES:]), g2_ref[...])
        y_ref[...] = (yn + bonus) * gate


def _scan_call(ps, vecs, mats, y_other, *, direction, tbs):
    bsz, t_len, _ = ps.shape
    nt = t_len // tbs
    reverse = direction == 1
    finalize = y_other is not None
    lora_w = RWKV_COLS - 3 * D_RWKV

    def tmap(i):
        return (nt - 1 - i) if reverse else i

    def col_spec(col):
        return pl.BlockSpec((None, tbs, D_RWKV), lambda b, i: (b, tmap(i), col))

    vec_spec = _const_spec((1, D_RWKV))
    mat_spec = _const_spec((LANES, D_RWKV))
    in_specs = [
        col_spec(0), col_spec(1), col_spec(2),
        pl.BlockSpec((None, tbs, lora_w), lambda b, i: (b, tmap(i), 3 * D_RWKV // lora_w)),
        vec_spec, mat_spec, vec_spec, mat_spec, vec_spec, vec_spec,
    ]
    o = 1 - direction
    args = [ps] * 4 + [vecs["w0"][direction], mats["w2p"][direction], vecs["a0"][direction],
                       mats["a2p"][direction], vecs["k_k"], vecs["k_a"]]
    if finalize:
        in_specs += [vec_spec, mat_spec, vec_spec, vec_spec, vec_spec, mat_spec, col_spec(0)]
        args += [vecs["r_k"], mats["g2"], vecs["lnx_g"], vecs["lnx_b"], vecs["a0"][o], mats["a2p"][o], y_other]
    kern = functools.partial(_scan_kernel, tbs=tbs, reverse=reverse, finalize=finalize)
    scratch = [pltpu.VMEM((N_PAIRS // 2, 2 * CHUNK, 2 * LANES), F32)]
    return pl.pallas_call(
        kern,
        grid=(bsz, nt),
        in_specs=in_specs,
        out_specs=col_spec(0),
        out_shape=jax.ShapeDtypeStruct((bsz, t_len, D_RWKV), F32),
        scratch_shapes=scratch,
        compiler_params=pltpu.CompilerParams(
            dimension_semantics=("arbitrary", "arbitrary"), vmem_limit_bytes=VMEM_LIMIT,
        ),
        name="rwkv7_scan_bwd" if reverse else "rwkv7_scan_fwd",
    )(*args)


S5_PAIR_LANES = 2 * S5_GROUP
S5_PAIRS_PER_STEP = LANES // S5_PAIR_LANES
S5_WIDTH = S5_CHUNK * S5_PAIR_LANES
S5_TOK_PER_VREG = LANES // S5_PAIR_LANES


def _gelu_tanh(x):
    return 0.5 * x * (1.0 + jnp.tanh(math.sqrt(2.0 / math.pi) * (x + 0.044715 * (x * x * x))))


def _s5_pair(u, wz, tz, cz, pw_ref, pj, dvec, wg, bg, nc, nlev):
    ub = u.astype(BF16)
    z = jnp.dot(ub, wz, preferred_element_type=F32)
    row = lax.broadcasted_iota(jnp.int32, (nc, LANES), 0)
    xs = [z[:, j * LANES:(j + 1) * LANES] for j in range(4)]

    def shifted(x, s, up):
        if up:
            return jnp.where(row >= nc - s, 0.0, pltpu.roll(x, nc - s, 0))
        return jnp.where(row < s, 0.0, pltpu.roll(x, s, 0))

    for lev in range(nlev):
        s = 1 << lev
        for d in range(2):
            xr, xi = xs[2 * d], xs[2 * d + 1]
            ar = pw_ref[pj, lev, 2 * d:2 * d + 1, :]
            ai = pw_ref[pj, lev, 2 * d + 1:2 * d + 2, :]
            pr = shifted(xr, s, d == 1)
            pi = shifted(xi, s, d == 1)
            xs[2 * d] = xr + (ar * pr - ai * pi)
            xs[2 * d + 1] = xi + (ar * pi + ai * pr)
    xin = jnp.concatenate([shifted(xs[0], 1, False), shifted(xs[1], 1, False),
                           shifted(xs[2], 1, True), shifted(xs[3], 1, True)], axis=1)
    y = (jnp.dot(ub, tz, preferred_element_type=F32)
         + jnp.dot(xin.astype(BF16), cz, preferred_element_type=F32)
         + dvec * u)
    zact = _gelu_tanh(y)
    gate = _sigmoid(jnp.dot(zact.astype(BF16), wg, preferred_element_type=F32) + bg)
    return zact * gate


def _s5_kernel(u_ref, wz_ref, tz_ref, cz_ref, pw_ref, d_ref, wg_ref, bg_ref, o_ref, *, nc, nlev):
    lane = lax.broadcasted_iota(jnp.int32, (nc, LANES), 1)
    seg = lane // S5_PAIR_LANES
    tok = [u_ref[pl.ds(s, nc, stride=S5_CHUNK), :] for s in range(S5_CHUNK)]
    outs = []
    for pj in range(S5_PAIRS_PER_STEP):
        cols = []
        for qv in range(S5_CHUNK // S5_TOK_PER_VREG):
            acc = None
            for i in range(S5_TOK_PER_VREG):
                x = tok[S5_TOK_PER_VREG * qv + i]
                sh = (S5_PAIR_LANES * (i - pj)) % LANES
                if sh:
                    x = pltpu.roll(x, sh, 1)
                acc = x if acc is None else jnp.where(seg == i, x, acc)
            cols.append(acc)
        u = jnp.concatenate(cols, axis=1)
        outs.append(_s5_pair(u, wz_ref[pj], tz_ref[pj], cz_ref[pj], pw_ref, pj, d_ref[pj], wg_ref[pj],
                             bg_ref[pj], nc, nlev))
    for s in range(S5_CHUNK):
        qv, i = divmod(s, S5_TOK_PER_VREG)
        acc = None
        for pj in range(S5_PAIRS_PER_STEP):
            x = outs[pj][:, qv * LANES:(qv + 1) * LANES]
            sh = (S5_PAIR_LANES * (pj - i)) % LANES
            if sh:
                x = pltpu.roll(x, sh, 1)
            acc = x if acc is None else jnp.where(seg == pj, x, acc)
        o_ref[pl.ds(s, nc, stride=S5_CHUNK), :] = acc


def _s5_call(u, tabs):
    bsz, t_len, width = u.shape
    nc = t_len // S5_CHUNK
    nsteps = width // LANES
    nlev = max(1, (nc - 1).bit_length())
    kern = functools.partial(_s5_kernel, nc=nc, nlev=nlev)
    pp = S5_PAIRS_PER_STEP

    def tab_spec(arr):
        nd = arr.ndim - 1
        return pl.BlockSpec((pp,) + arr.shape[1:], lambda j, b: (j,) + (0,) * nd)

    names = ["wz", "tz", "cz", "pw", "dvec", "wglu", "bglu"]
    return pl.pallas_call(
        kern,
        grid=(nsteps, bsz),
        in_specs=[pl.BlockSpec((None, t_len, LANES), lambda j, b: (b, 0, j))] + [tab_spec(tabs[n]) for n in names],
        out_specs=pl.BlockSpec((None, t_len, LANES), lambda j, b: (b, 0, j)),
        out_shape=jax.ShapeDtypeStruct(u.shape, F32),
        compiler_params=pltpu.CompilerParams(vmem_limit_bytes=VMEM_LIMIT),
        name="s5_mixer",
    )(u, *[tabs[n] for n in names])


def _s5_tables(lam_re, lam_im, log_dt, b_re, b_im, c_re, c_im, d_skip, w_glu, b_glu, max_levels):
    hp = lax.Precision.HIGHEST
    L, G, P, H = S5_CHUNK, N_S5_GROUPS, S5_STATE, S5_GROUP
    dt = jnp.exp(log_dt)[:, :, None]
    are, aim = lam_re * dt, lam_im * dt

    def apow(n):
        n = jnp.asarray(n, F32)
        shape = n.shape + (1, 1, 1)
        mag = jnp.exp(n.reshape(shape) * are)
        ang = n.reshape(shape) * aim
        return mag * jnp.cos(ang), mag * jnp.sin(ang)

    a1r, a1i = apow(jnp.ones(()))
    den = lam_re * lam_re + lam_im * lam_im
    qr = ((a1r - 1.0) * lam_re + a1i * lam_im) / den
    qi = (a1i * lam_re - (a1r - 1.0) * lam_im) / den
    bbr = qr[..., None] * b_re - qi[..., None] * b_im
    bbi = qr[..., None] * b_im + qi[..., None] * b_re
    pr, pi = apow(jnp.arange(L + 1))

    abr = pr[..., None] * bbr - pi[..., None] * bbi
    abi = pr[..., None] * bbi + pi[..., None] * bbr
    s_idx = jnp.arange(L)
    wf_r = jnp.transpose(abr[L - 1 - s_idx, 0], (1, 0, 3, 2))
    wf_i = jnp.transpose(abi[L - 1 - s_idx, 0], (1, 0, 3, 2))
    wb_r = jnp.transpose(abr[s_idx, 1], (1, 0, 3, 2))
    wb_i = jnp.transpose(abi[s_idx, 1], (1, 0, 3, 2))
    kf = (jnp.einsum("ghp,ngpi->nghi", c_re[0], abr[:L, 0], precision=hp)
          - jnp.einsum("ghp,ngpi->nghi", c_im[0], abi[:L, 0], precision=hp))
    kb = (jnp.einsum("ghp,ngpi->nghi", c_re[1], abr[:L, 1], precision=hp)
          - jnp.einsum("ghp,ngpi->nghi", c_im[1], abi[:L, 1], precision=hp))
    lag = s_idx[None, :] - s_idx[:, None]
    tf = jnp.where((lag >= 0)[:, :, None, None, None], kf[jnp.clip(lag, 0, L - 1)], 0.0)
    tb = jnp.where((lag <= 0)[:, :, None, None, None], kb[jnp.clip(-lag, 0, L - 1)], 0.0)
    tz_g = jnp.transpose(tf + tb, (2, 0, 4, 1, 3))
    t1 = s_idx + 1
    cf_r = (c_re[0][:, None] * pr[t1, 0].transpose(1, 0, 2)[:, :, None, :]
            - c_im[0][:, None] * pi[t1, 0].transpose(1, 0, 2)[:, :, None, :])
    cf_i = (c_re[0][:, None] * pi[t1, 0].transpose(1, 0, 2)[:, :, None, :]
            + c_im[0][:, None] * pr[t1, 0].transpose(1, 0, 2)[:, :, None, :])
    t2 = L - s_idx
    cb_r = (c_re[1][:, None] * pr[t2, 1].transpose(1, 0, 2)[:, :, None, :]
            - c_im[1][:, None] * pi[t2, 1].transpose(1, 0, 2)[:, :, None, :])
    cb_i = (c_re[1][:, None] * pi[t2, 1].transpose(1, 0, 2)[:, :, None, :]
            + c_im[1][:, None] * pr[t2, 1].transpose(1, 0, 2)[:, :, None, :])

    eye2 = jnp.eye(2, dtype=F32)
    g2 = G // 2

    def pair_sq(m):
        m = m.reshape(g2, 2, L, H, L, H)
        m = m[:, :, :, :, :, None, :] * eye2[None, :, None, None, None, :, None]
        return jnp.transpose(m, (0, 2, 1, 3, 4, 5, 6)).reshape(g2, L * 2 * H, L * 2 * H)

    def pair_in(m):
        m = m.reshape(g2, 2, L, H, P)
        m = m[:, :, :, :, None, :] * eye2[None, :, None, None, :, None]
        return jnp.transpose(m, (0, 2, 1, 3, 4, 5)).reshape(g2, L * 2 * H, 2 * P)

    def pair_out(m):
        m = jnp.transpose(m, (0, 3, 1, 2)).reshape(g2, 2, P, L, H)
        m = m[:, :, :, :, None, :] * eye2[None, :, None, None, :, None]
        return m.reshape(g2, 2 * P, L * 2 * H)

    wz = jnp.concatenate([pair_in(m) for m in (wf_r, wf_i, wb_r, wb_i)], axis=2)
    cz = jnp.concatenate([pair_out(m) for m in (cf_r, -cf_i, cb_r, -cb_i)], axis=1)
    tz = pair_sq(tz_g)
    eye_l = jnp.eye(L, dtype=F32)
    wglu = pair_sq(eye_l[None, :, None, :, None] * w_glu[:, None, :, None, :])

    def lanes_gh(vec):
        return jnp.broadcast_to(vec.reshape(g2, 1, 2, H), (g2, L, 2, H)).reshape(g2, 1, L * 2 * H)

    lev_n = (L * (2 ** jnp.arange(max_levels))).astype(F32)
    lr, li = apow(lev_n)

    def lanes_gp(m):
        return jnp.transpose(m.reshape(max_levels, G // 2, 2 * P), (1, 0, 2))[:, :, None, :]

    pw = jnp.concatenate([lanes_gp(lr[:, 0]), lanes_gp(li[:, 0]), lanes_gp(lr[:, 1]), lanes_gp(li[:, 1])],
                         axis=2)
    return {
        "wz": wz.astype(BF16), "tz": tz.astype(BF16), "cz": cz.astype(BF16), "pw": pw,
        "dvec": lanes_gh(d_skip), "wglu": wglu.astype(BF16), "bglu": lanes_gh(b_glu),
    }


def _rms(x):
    return x * lax.rsqrt(jnp.mean(x * x, axis=-1, keepdims=True) + RMS_EPS)


def _out_kernel(x_ref, yr_ref, zs_ref, mod_ref, sg_ref, wo_ref, n2_ref, w1_ref, w3_ref, w2_ref, fg_ref, o_ref):
    gate1 = mod_ref[2:3, :]
    shift2 = mod_ref[3:4, :]
    scale2 = mod_ref[4:5, :]
    gate2 = mod_ref[5:6, :]
    ys = _rms(zs_ref[...]) * sg_ref[...]
    mix = (jnp.dot(yr_ref[...].astype(BF16), wo_ref[:D_RWKV, :], preferred_element_type=F32)
           + jnp.dot(ys.astype(BF16), wo_ref[D_RWKV:, :], preferred_element_type=F32))
    x1 = x_ref[...] + gate1 * mix
    h = ((_rms(x1) * n2_ref[...]) * (1.0 + scale2) + shift2).astype(BF16)
    f1 = jnp.dot(h, w1_ref[...], preferred_element_type=F32)
    f3 = jnp.dot(h, w3_ref[...], preferred_element_type=F32)
    act = (f1 * _sigmoid(f1)) * f3
    f = jnp.dot(act.astype(BF16), w2_ref[...], preferred_element_type=F32)
    x2 = x1 + gate2 * f
    o_ref[...] = _rms(x2) * fg_ref[...]


def _out_call(x, y_rwkv, z_s5, mod3, s5_out_g, w_out, norm2_g, w_ff1, w_ff3, w_ff2, final_g, tb):
    bsz, t_len, _ = x.shape

    def resident(shape):
        return pl.BlockSpec(shape, lambda b, i: (0, 0), pipeline_mode=pl.Buffered(1))

    return pl.pallas_call(
        _out_kernel,
        grid=(bsz, t_len // tb),
        in_specs=[
            pl.BlockSpec((None, tb, D_MODEL), lambda b, i: (b, i, 0)),
            pl.BlockSpec((None, tb, D_RWKV), lambda b, i: (b, i, 0)),
            pl.BlockSpec((None, tb, D_S5), lambda b, i: (b, i, 0)),
            pl.BlockSpec((None, 6, D_MODEL), lambda b, i: (b, 0, 0)),
            _const_spec((1, D_S5)),
            resident((D_MODEL, D_MODEL)),
            _const_spec((1, D_MODEL)),
            resident((D_MODEL, D_FF)),
            resident((D_MODEL, D_FF)),
            resident((D_FF, D_MODEL)),
            _const_spec((1, D_MODEL)),
        ],
        out_specs=pl.BlockSpec((None, tb, D_MODEL), lambda b, i: (b, i, 0)),
        out_shape=jax.ShapeDtypeStruct(x.shape, F32),
        compiler_params=pltpu.CompilerParams(vmem_limit_bytes=VMEM_LIMIT),
        name="out_proj_ffn",
    )(x, y_rwkv, z_s5, mod3, s5_out_g, w_out, norm2_g, w_ff1, w_ff3, w_ff2, final_g)


def _block_sizes(t_len):
    tb_in = min(512, t_len)
    tbs = min(256, t_len)
    tb_out = min(256, t_len)
    return tb_in, tbs, tb_out


def _pad_lora(w, n_in):
    z = jnp.zeros_like(w[0])
    return [jnp.concatenate([w[0], z], axis=0).astype(BF16), jnp.concatenate([z, w[1]], axis=0).astype(BF16)]


def _trunk(x, mod3, prm, s5_tabs):
    bsz, t_len, _ = x.shape
    tb_in, tbs, tb_out = _block_sizes(t_len)
    ps, u = _in_call(x, mod3, prm["norm1_g"], prm["w_in"], prm["mu_shift"], tb_in)
    y_b = _scan_call(ps, prm["vecs"], prm["mats"], None, direction=1, tbs=tbs)
    y_rwkv = _scan_call(ps, prm["vecs"], prm["mats"], y_b, direction=0, tbs=tbs)
    z_s5 = _s5_call(u, s5_tabs)
    return _out_call(x, y_rwkv, z_s5, mod3, prm["s5_out_g"], prm["w_out"], prm["norm2_g"],
                     prm["w_ff1"], prm["w_ff3"], prm["w_ff2"], prm["final_g"], tb_out)


def _prepare(norm1_g, w_in, mu_shift, w0, w2, a0, a2, g2, k_k, k_a, r_k, lnx_g, lnx_b, s5_out_g, w_out,
             norm2_g, w_ff1, w_ff3, w_ff2, final_g):
    row = lambda v: v.reshape(1, -1)
    vecs = {
        "w0": [row(w0[0]), row(w0[1])], "a0": [row(a0[0]), row(a0[1])],
        "k_k": row(k_k), "k_a": row(k_a), "r_k": row(r_k), "lnx_g": row(lnx_g), "lnx_b": row(lnx_b),
    }
    mats = {"w2p": _pad_lora(w2, 64), "a2p": _pad_lora(a2, 64), "g2": g2.astype(BF16)}
    return {
        "norm1_g": row(norm1_g), "w_in": w_in.astype(BF16), "mu_shift": row(mu_shift),
        "vecs": vecs, "mats": mats, "s5_out_g": row(s5_out_g), "w_out": w_out.astype(BF16),
        "norm2_g": row(norm2_g), "w_ff1": w_ff1.astype(BF16), "w_ff3": w_ff3.astype(BF16),
        "w_ff2": w_ff2.astype(BF16), "final_g": row(final_g),
    }


def kernel(x_prompt, x_sample, c_prompt, c_sample, norm1_g, w_ada, b_ada, w_in, mu_shift, w0, w2, a0, a2, g2,
           k_k, k_a, r_k, lnx_g, lnx_b, lam_re, lam_im, log_dt, b_re, b_im, c_re, c_im, d_skip, w_glu, b_glu,
           s5_out_g, w_out, norm2_g, w_ff1, w_ff3, w_ff2, final_g):
    depth = norm1_g.shape[0]
    assert depth == 1, "the fused output kernel applies the final norm, so it supports a single layer"
    nbp = x_prompt.shape[0]
    xs = [x_prompt, x_sample]
    c_all = jnp.concatenate([c_prompt, c_sample], axis=0)
    max_t = max(x_prompt.shape[1], x_sample.shape[1])
    max_levels = max(1, (max_t // S5_CHUNK - 1).bit_length())
    for i in range(depth):
        prm = _prepare(norm1_g[i], w_in[i], mu_shift[i], w0[i], w2[i], a0[i], a2[i], g2[i], k_k[i], k_a[i],
                       r_k[i], lnx_g[i], lnx_b[i], s5_out_g[i], w_out[i], norm2_g[i], w_ff1[i], w_ff3[i],
                       w_ff2[i], final_g)
        s5_tabs = _s5_tables(lam_re[i], lam_im[i], log_dt[i], b_re[i], b_im[i], c_re[i], c_im[i],
                             d_skip[i], w_glu[i], b_glu[i], max_levels)
        mod = _mod_call(c_all, w_ada[i].astype(BF16), b_ada[i].reshape(1, -1))
        mod3 = mod.reshape(c_all.shape[0], 6, D_MODEL)
        xs = [_trunk(xs[0], mod3[:nbp], prm, s5_tabs), _trunk(xs[1], mod3[nbp:], prm, s5_tabs)]
    return tuple(xs)
```

```python
import functools
import math

import jax
import jax.numpy as jnp
import numpy as np
from jax import lax
from jax.experimental import pallas as pl
from jax.experimental.pallas import tpu as pltpu

F32 = jnp.float32
BF16 = jnp.bfloat16

D_MODEL = 1024
D_RWKV = 512
HEAD_DIM = 64
D_S5 = 512
S5_GROUP = 16
N_S5_GROUPS = 32
S5_STATE = 64
D_FF = 2816
RWKV_COLS = 1920
D_IN_PROJ = 2432
RMS_EPS = 1e-6
LNX_EPS = 64e-5

LANES = 128
CHUNK = 64
S5_CHUNK = 16
N_PAIRS = D_RWKV // LANES
N_S5_PAIRS = N_S5_GROUPS // 2
VMEM_LIMIT = 56 * 1024 * 1024


def _dot(a, b):
    return jnp.dot(a.astype(BF16), b.astype(BF16), preferred_element_type=F32)


def _split2(x):
    hi = x.astype(BF16)
    lo = (x - hi.astype(F32)).astype(BF16)
    return hi, lo


def _split3(x):
    hi = x.astype(BF16)
    r1 = x - hi.astype(F32)
    mid = r1.astype(BF16)
    lo = (r1 - mid.astype(F32)).astype(BF16)
    return hi, mid, lo


def _dot_exact_lhs(m, x):
    mh = m.astype(BF16)
    h, mid, lo = _split3(x)
    d = functools.partial(jnp.dot, preferred_element_type=F32)
    return d(mh, h) + (d(mh, mid) + d(mh, lo))


def _sigmoid(x):
    return 1.0 / (1.0 + jnp.exp(-x))


def _softplus(x):
    return jnp.maximum(x, 0.0) + jnp.log(1.0 + jnp.exp(-jnp.abs(x)))


def _const_spec(shape):
    nd = len(shape)
    return pl.BlockSpec(shape, lambda *_: (0,) * nd)


def _mod_kernel(c_ref, w_ref, b_ref, o_ref):
    c = c_ref[...]
    s = c * _sigmoid(c)
    o_ref[...] = _dot(s, w_ref[...]) + b_ref[...]


def _mod_call(c_all, w_ada, b_ada):
    nb = c_all.shape[0]
    ncol = w_ada.shape[1] // D_MODEL
    return pl.pallas_call(
        _mod_kernel,
        grid=(ncol,),
        in_specs=[
            pl.BlockSpec((nb, D_MODEL), lambda j: (0, 0)),
            pl.BlockSpec((D_MODEL, D_MODEL), lambda j: (0, j)),
            pl.BlockSpec((1, D_MODEL), lambda j: (0, j)),
        ],
        out_specs=pl.BlockSpec((nb, D_MODEL), lambda j: (0, j)),
        out_shape=jax.ShapeDtypeStruct((nb, w_ada.shape[1]), F32),
        name="adaln_mod",
    )(c_all, w_ada, b_ada)


HALO = 8


def _in_kernel(x_ref, xp_ref, xn_ref, mod_ref, g_ref, w_ref, mu_ref, ps_ref, u_ref, p_scr, *, tb, nt):
    i = pl.program_id(1)
    shift = mod_ref[0:1, :]
    scale = mod_ref[1:2, :]
    g = g_ref[...]
    xa = jnp.concatenate([xp_ref[...], x_ref[...], xn_ref[...]], axis=0)
    ms = jnp.mean(xa * xa, axis=-1, keepdims=True)
    h = ((xa * lax.rsqrt(ms + RMS_EPS)) * g) * (1.0 + scale) + shift
    hb = h.astype(BF16)
    p_scr[...] = jnp.dot(hb, w_ref[:, :RWKV_COLS], preferred_element_type=F32)
    u_ref[...] = jnp.dot(hb, w_ref[:, RWKV_COLS:], preferred_element_type=F32)[HALO:HALO + tb]
    p = p_scr[HALO:HALO + tb, :]
    prev = p_scr[HALO - 1:HALO - 1 + tb, :]
    nxt = p_scr[HALO + 1:HALO + 1 + tb, :]
    row = lax.broadcasted_iota(jnp.int32, (tb, 1), 0)
    prev = jnp.where(jnp.logical_and(row == 0, i == 0), 0.0, prev)
    nxt = jnp.where(jnp.logical_and(row == tb - 1, i == nt - 1), 0.0, nxt)
    ps_ref[...] = p + mu_ref[...] * (0.5 * (prev + nxt) - p)


def _in_call(x, mod3, norm1_g, w_in, mu_shift, tb):
    bsz, t_len, _ = x.shape
    nt = t_len // tb
    hb = tb // HALO
    last = t_len // HALO - 1
    kern = functools.partial(_in_kernel, tb=tb, nt=nt)
    return pl.pallas_call(
        kern,
        grid=(bsz, nt),
        in_specs=[
            pl.BlockSpec((None, tb, D_MODEL), lambda b, i: (b, i, 0)),
            pl.BlockSpec((None, HALO, D_MODEL), lambda b, i: (b, jnp.maximum(i * hb - 1, 0), 0)),
            pl.BlockSpec((None, HALO, D_MODEL), lambda b, i: (b, jnp.minimum((i + 1) * hb, last), 0)),
            pl.BlockSpec((None, 6, D_MODEL), lambda b, i: (b, 0, 0)),
            _const_spec((1, D_MODEL)),
            pl.BlockSpec((D_MODEL, D_IN_PROJ), lambda b, i: (0, 0), pipeline_mode=pl.Buffered(1)),
            _const_spec((1, RWKV_COLS)),
        ],
        out_specs=[
            pl.BlockSpec((None, tb, RWKV_COLS), lambda b, i: (b, i, 0)),
            pl.BlockSpec((None, tb, D_S5), lambda b, i: (b, i, 0)),
        ],
        out_shape=[
            jax.ShapeDtypeStruct((bsz, t_len, RWKV_COLS), F32),
            jax.ShapeDtypeStruct((bsz, t_len, D_S5), F32),
        ],
        scratch_shapes=[pltpu.VMEM((tb + 2 * HALO, RWKV_COLS), F32)],
        compiler_params=pltpu.CompilerParams(vmem_limit_bytes=VMEM_LIMIT),
        name="in_proj_shift",
    )(x, x, x, mod3, norm1_g, w_in, mu_shift)


def _head_sum(x, blk):
    h, lo = _split2(x)
    d = functools.partial(jnp.dot, preferred_element_type=F32)
    return d(h, blk) + d(lo, blk)


def _bdot(a, b):
    return jnp.dot(a, b, preferred_element_type=F32)


def _side_by_side_diag(x):
    z = jnp.zeros((x.shape[0], LANES), x.dtype)
    return jnp.concatenate([jnp.concatenate([x[:, :LANES], z], axis=1),
                            jnp.concatenate([z, x[:, LANES:]], axis=1)], axis=0)


def _scan_masks(tbs, reverse):
    two = 2 * CHUNK
    ri = np.arange(two)[:, None]
    ci = np.arange(2 * LANES)[None, :] & (LANES - 1)
    before = (ci > ri) if reverse else (ci < ri)
    levels = [((ri >> 1) == (ci >> 1)) & before]
    sh = 1
    while (1 << sh) < CHUNK:
        levels.append(((ri >> (sh + 1)) == (ci >> (sh + 1))) & ((ri >> sh) != (ci >> sh)) & before)
        sh += 1
    rt, ct = np.arange(tbs)[:, None], np.arange(tbs)[None, :]
    tri = ((rt >> 6) == (ct >> 6)) & ((ct >= rt) if reverse else (ct <= rt))
    hb = np.arange(D_RWKV)
    blk = (hb[:, None] >> 6) == (hb[None, :] >> 6)
    return {
        "eye": jnp.asarray((ri == ci).astype(np.float32)),
        "levels": jnp.asarray(np.stack(levels).astype(np.float32), dtype=BF16),
        "tri": jnp.asarray(tri.astype(np.float32), dtype=BF16),
        "blk": jnp.asarray(blk.astype(np.float32), dtype=BF16),
    }


def _tri_inverse_batch(n_raw, eye, levels_ref, reverse):
    two = 2 * CHUNK
    n_sbs = [jnp.concatenate([n_raw[i], n_raw[i + 1]], axis=1) for i in range(0, len(n_raw), 2)]
    n_bf = [n.astype(BF16) for n in n_sbs]
    m1 = levels_ref[0]
    t = [eye + (n * m1).astype(F32) for n in n_bf]
    sh = 1
    while (1 << sh) < CHUNK:
        mk = levels_ref[sh]
        cm = [_side_by_side_diag(n * mk) for n in n_bf]
        tb = [x.astype(BF16) for x in t]
        half = 1 << sh
        if half % 8 == 0:
            off = 0 if reverse else half
            starts = [b0 + off for b0 in range(0, two, 2 * half)]
            pick = lambda x: jnp.concatenate([x[s0:s0 + half] for s0 in starts], axis=0)
            xs = [_bdot(pick(a), c).astype(BF16) for a, c in zip(tb, cm)]
            upd = [_bdot(a, _side_by_side_diag(b)) for a, b in zip(xs, tb)]
            new_t = []
            for x, u in zip(t, upd):
                parts = []
                for i, s0 in enumerate(starts):
                    changed = x[s0:s0 + half] + u[i * half:(i + 1) * half]
                    keep = x[s0 - off + (half - off):s0 - off + (half - off) + half]
                    parts += [changed, keep] if reverse else [keep, changed]
                new_t.append(jnp.concatenate(parts, axis=0))
            t = new_t
        else:
            xs = [_bdot(a, c).astype(BF16) for a, c in zip(tb, cm)]
            t = [x + _bdot(a, _side_by_side_diag(b)) for x, a, b in zip(t, xs, tb)]
        sh += 1
    out = []
    for t0 in t:
        out += [t0[:, :LANES], t0[:, LANES:]]
    return out


def _scan_kernel(*refs, tbs, reverse, finalize):
    if finalize:
        (r_ref, k_ref, v_ref, lora_ref, w0_ref, w2_ref, a0_ref, a2_ref, kk_ref, ka_ref,
         eye_ref, levels_ref, tri_ref, blk_ref,
         rk_ref, g2_ref, lg_ref, lb_ref, a0o_ref, a2o_ref, yo_ref, y_ref, st_scr) = refs
    else:
        (r_ref, k_ref, v_ref, lora_ref, w0_ref, w2_ref, a0_ref, a2_ref, kk_ref, ka_ref,
         eye_ref, levels_ref, tri_ref, blk_ref, y_ref, st_scr) = refs

    @pl.when(pl.program_id(1) == 0)
    def _():
        st_scr[...] = jnp.zeros_like(st_scr)

    two = 2 * CHUNK
    ri = lax.broadcasted_iota(jnp.int32, (two, two), 0)
    ci = lax.broadcasted_iota(jnp.int32, (two, two), 1)
    same_head = (ri >> 6) == (ci >> 6)
    before = (ci > ri) if reverse else (ci < ri)
    bd_strict = jnp.logical_and(same_head, before)
    bd_incl = jnp.logical_and(same_head, jnp.logical_or(before, ri == ci))
    lane = lax.broadcasted_iota(jnp.int32, (CHUNK, LANES), 1)
    head0 = lane < HEAD_DIM
    tri = tri_ref[...]
    blk = blk_ref[...]

    def stack(x):
        return jnp.concatenate([jnp.where(head0, x, 0.0), jnp.where(head0, 0.0, x)], axis=0)

    k = k_ref[...]
    xa = lora_ref[:, LANES:2 * LANES]
    zw = w0_ref[...] + _dot(jnp.tanh(lora_ref[:, :LANES]), w2_ref[...])
    wd = -_softplus(-zw) - 0.5
    lw = -jnp.exp(wd)
    a = _sigmoid(a0_ref[...] + _dot(xa, a2_ref[...]))
    ka = ka_ref[...]
    kd = k * (1.0 + (a - 1.0) * ka)
    kkr = k * kk_ref[...]
    kkn = kkr * lax.rsqrt(jnp.maximum(_head_sum(kkr * kkr, blk), 1e-24))

    n_chunks = tbs // CHUNK
    units = [(ch, q) for ch in range(n_chunks) for q in range(N_PAIRS)]

    c_blk = _dot_exact_lhs(tri, lw)
    r_blk, v_blk = r_ref[...], v_ref[...]
    pre, n_raw = {}, []
    for ch, q in units:
        rows = slice(ch * CHUNK, (ch + 1) * CHUNK)
        ls = slice(q * LANES, (q + 1) * LANES)
        c, lwc = c_blk[rows, ls], lw[rows, ls]
        ctot = c[0:1, :] if reverse else c[CHUNK - 1:CHUNK, :]
        eg = jnp.exp(c)
        ei = jnp.exp(-c)
        ex = jnp.exp(c - lwc)
        ee = jnp.exp(ctot - c)
        akk = a[rows, ls] * kkn[rows, ls]
        kdc = kd[rows, ls]
        rt = r_blk[rows, ls] * eg
        at = -(kkn[rows, ls] * ex)
        g = lax.dot_general(jnp.concatenate([at, rt], axis=0).astype(BF16),
                            jnp.concatenate([stack(akk * ei), stack(kdc * ei)], axis=0).astype(BF16),
                            (((1,), (1,)), ((), ())), preferred_element_type=F32)
        n_raw.append(stack(g[:CHUNK, :two]))
        pre[ch, q] = dict(
            rt_s=stack(rt).astype(BF16), at_s=stack(at).astype(BF16), v_s=stack(v_blk[rows, ls]).astype(BF16),
            aak=jnp.where(bd_strict, stack(g[:CHUNK, two:]), 0.0).astype(BF16),
            ark=jnp.where(bd_incl, stack(g[CHUNK:, two:]), 0.0).astype(BF16),
            arb=jnp.where(bd_incl, stack(g[CHUNK:, :two]), 0.0).astype(BF16),
            btet=jnp.transpose(stack(akk * ee)).astype(BF16),
            ktet=jnp.transpose(stack(kdc * ee)).astype(BF16),
            gcol=jnp.transpose(jnp.broadcast_to(jnp.exp(ctot), (two, LANES))),
        )
    t_inv = _tri_inverse_batch(n_raw, eye_ref[...], levels_ref, reverse)

    def sbs(ch, m, name):
        return jnp.concatenate([pre[ch, 2 * m][name], pre[ch, 2 * m + 1][name]], axis=1)

    half = N_PAIRS // 2
    duo = {}
    for ch in range(n_chunks):
        for m in range(half):
            lhs = jnp.concatenate([sbs(ch, m, "aak"), sbs(ch, m, "ark"), sbs(ch, m, "ktet")], axis=0)
            avk = _bdot(lhs, _side_by_side_diag(sbs(ch, m, "v_s")))
            wms, ulocs = [], []
            for j in range(2):
                q = 2 * m + j
                av = avk[:two, j * LANES:(j + 1) * LANES].astype(BF16)
                rhs = jnp.concatenate([pre[ch, q]["at_s"], av], axis=1)
                wu = _bdot(t_inv[ch * N_PAIRS + q].astype(BF16), rhs)
                wms.append(wu[:, :LANES].astype(BF16))
                ulocs.append(wu[:, LANES:])
            duo[ch, m] = dict(
                lhs_p=jnp.concatenate([jnp.concatenate(wms, axis=1), sbs(ch, m, "rt_s")], axis=0),
                lhs_u=jnp.concatenate([sbs(ch, m, "arb"), sbs(ch, m, "btet")], axis=0),
                uloc=jnp.concatenate(ulocs, axis=1), yloc=avk[two:2 * two], kvt=avk[2 * two:],
                gcol=sbs(ch, m, "gcol"),
            )

    st = [st_scr[m] for m in range(half)]
    order = range(n_chunks - 1, -1, -1) if reverse else range(n_chunks)
    for ch in order:
        ys = []
        for m in range(half):
            d = duo[ch, m]
            p = _bdot(d["lhs_p"], _side_by_side_diag(st[m].astype(BF16)))
            u = _side_by_side_diag((p[:two] + d["uloc"]).astype(BF16))
            ab = _bdot(d["lhs_u"], u)
            y_s = p[two:] + ab[:two] + d["yloc"]
            ys.append(y_s[:CHUNK] + y_s[CHUNK:])
            st[m] = st[m] * d["gcol"] + ab[two:] + d["kvt"]
        y_ref[ch * CHUNK:(ch + 1) * CHUNK, :] = jnp.concatenate(ys, axis=1)
    for m in range(half):
        st_scr[m] = st[m]

    if finalize:
        y = y_ref[...] + yo_ref[...]
        inv_n = 1.0 / HEAD_DIM
        mean = _head_sum(y, blk) * inv_n
        d = y - mean
        var = _head_sum(d * d, blk) * inv_n
        yn = d * lax.rsqrt(var + LNX_EPS) * lg_ref[...] + lb_ref[...]
        a_o = _sigmoid(a0o_ref[...] + _dot(xa, a2o_ref[...]))
        kd_o = k * (1.0 + (a_o - 1.0) * ka)
        rr = r_ref[...] * rk_ref[...]
        bonus = (_head_sum(rr * kd, blk) + _head_sum(rr * kd_o, blk)) * v_ref[...]
        gate = _dot(_sigmoid(lora_ref[:, 2 * LANES:]), g2_ref[...])
        y_ref[...] = (yn + bonus) * gate


def _scan_call(ps, vecs, mats, y_other, *, direction, tbs):
    bsz, t_len, _ = ps.shape
    nt = t_len // tbs
    reverse = direction == 1
    finalize = y_other is not None
    lora_w = RWKV_COLS - 3 * D_RWKV

    def tmap(i):
        return (nt - 1 - i) if reverse else i

    def col_spec(col):
        return pl.BlockSpec((None, tbs, D_RWKV), lambda b, i: (b, tmap(i), col))

    vec_spec = _const_spec((1, D_RWKV))
    mat_spec = _const_spec((LANES, D_RWKV))
    in_specs = [
        col_spec(0), col_spec(1), col_spec(2),
        pl.BlockSpec((None, tbs, lora_w), lambda b, i: (b, tmap(i), 3 * D_RWKV // lora_w)),
        vec_spec, mat_spec, vec_spec, mat_spec, vec_spec, vec_spec,
    ]
    o = 1 - direction
    args = [ps] * 4 + [vecs["w0"][direction], mats["w2p"][direction], vecs["a0"][direction],
                       mats["a2p"][direction], vecs["k_k"], vecs["k_a"]]
    masks = _scan_masks(tbs, reverse)
    for name in ("eye", "levels", "tri", "blk"):
        in_specs.append(_const_spec(masks[name].shape))
        args.append(masks[name])
    if finalize:
        in_specs += [vec_spec, mat_spec, vec_spec, vec_spec, vec_spec, mat_spec, col_spec(0)]
        args += [vecs["r_k"], mats["g2"], vecs["lnx_g"], vecs["lnx_b"], vecs["a0"][o], mats["a2p"][o], y_other]
    kern = functools.partial(_scan_kernel, tbs=tbs, reverse=reverse, finalize=finalize)
    scratch = [pltpu.VMEM((N_PAIRS // 2, 2 * CHUNK, 2 * LANES), F32)]
    return pl.pallas_call(
        kern,
        grid=(bsz, nt),
        in_specs=in_specs,
        out_specs=col_spec(0),
        out_shape=jax.ShapeDtypeStruct((bsz, t_len, D_RWKV), F32),
        scratch_shapes=scratch,
        compiler_params=pltpu.CompilerParams(
            dimension_semantics=("arbitrary", "arbitrary"), vmem_limit_bytes=VMEM_LIMIT,
        ),
        name="rwkv7_scan_bwd" if reverse else "rwkv7_scan_fwd",
    )(*args)


S5_PAIR_LANES = 2 * S5_GROUP
S5_PAIRS_PER_STEP = LANES // S5_PAIR_LANES
S5_WIDTH = S5_CHUNK * S5_PAIR_LANES
S5_TOK_PER_VREG = LANES // S5_PAIR_LANES


def _gelu_tanh(x):
    return 0.5 * x * (1.0 + jnp.tanh(math.sqrt(2.0 / math.pi) * (x + 0.044715 * (x * x * x))))


def _s5_pair(u, wz, tz, cz, pw_ref, pj, dvec, wg, bg, nc, nlev):
    ub = u.astype(BF16)
    z = jnp.dot(ub, wz, preferred_element_type=F32)
    row = lax.broadcasted_iota(jnp.int32, (nc, LANES), 0)
    xs = [z[:, j * LANES:(j + 1) * LANES] for j in range(4)]

    def shifted(x, s, up):
        if up:
            return jnp.where(row >= nc - s, 0.0, pltpu.roll(x, nc - s, 0))
        return jnp.where(row < s, 0.0, pltpu.roll(x, s, 0))

    for lev in range(nlev):
        s = 1 << lev
        for d in range(2):
            xr, xi = xs[2 * d], xs[2 * d + 1]
            ar = pw_ref[pj, lev, 2 * d:2 * d + 1, :]
            ai = pw_ref[pj, lev, 2 * d + 1:2 * d + 2, :]
            pr = shifted(xr, s, d == 1)
            pi = shifted(xi, s, d == 1)
            xs[2 * d] = xr + (ar * pr - ai * pi)
            xs[2 * d + 1] = xi + (ar * pi + ai * pr)
    xin = jnp.concatenate([shifted(xs[0], 1, False), shifted(xs[1], 1, False),
                           shifted(xs[2], 1, True), shifted(xs[3], 1, True)], axis=1)
    y = (jnp.dot(ub, tz, preferred_element_type=F32)
         + jnp.dot(xin.astype(BF16), cz, preferred_element_type=F32)
         + dvec * u)
    zact = _gelu_tanh(y)
    gate = _sigmoid(jnp.dot(zact.astype(BF16), wg, preferred_element_type=F32) + bg)
    return zact * gate


def _s5_kernel(u_ref, wz_ref, tz_ref, cz_ref, pw_ref, d_ref, wg_ref, bg_ref, o_ref, *, nc, nlev):
    lane = lax.broadcasted_iota(jnp.int32, (nc, LANES), 1)
    seg = lane // S5_PAIR_LANES
    tok = [u_ref[pl.ds(s, nc, stride=S5_CHUNK), :] for s in range(S5_CHUNK)]
    outs = []
    for pj in range(S5_PAIRS_PER_STEP):
        cols = []
        for qv in range(S5_CHUNK // S5_TOK_PER_VREG):
            acc = None
            for i in range(S5_TOK_PER_VREG):
                x = tok[S5_TOK_PER_VREG * qv + i]
                sh = (S5_PAIR_LANES * (i - pj)) % LANES
                if sh:
                    x = pltpu.roll(x, sh, 1)
                acc = x if acc is None else jnp.where(seg == i, x, acc)
            cols.append(acc)
        u = jnp.concatenate(cols, axis=1)
        outs.append(_s5_pair(u, wz_ref[pj], tz_ref[pj], cz_ref[pj], pw_ref, pj, d_ref[pj], wg_ref[pj],
                             bg_ref[pj], nc, nlev))
    for s in range(S5_CHUNK):
        qv, i = divmod(s, S5_TOK_PER_VREG)
        acc = None
        for pj in range(S5_PAIRS_PER_STEP):
            x = outs[pj][:, qv * LANES:(qv + 1) * LANES]
            sh = (S5_PAIR_LANES * (pj - i)) % LANES
            if sh:
                x = pltpu.roll(x, sh, 1)
            acc = x if acc is None else jnp.where(seg == pj, x, acc)
        o_ref[pl.ds(s, nc, stride=S5_CHUNK), :] = acc


def _s5_call(u, tabs):
    bsz, t_len, width = u.shape
    nc = t_len // S5_CHUNK
    nsteps = width // LANES
    nlev = max(1, (nc - 1).bit_length())
    kern = functools.partial(_s5_kernel, nc=nc, nlev=nlev)
    pp = S5_PAIRS_PER_STEP

    def tab_spec(arr):
        nd = arr.ndim - 1
        return pl.BlockSpec((pp,) + arr.shape[1:], lambda j, b: (j,) + (0,) * nd)

    names = ["wz", "tz", "cz", "pw", "dvec", "wglu", "bglu"]
    return pl.pallas_call(
        kern,
        grid=(nsteps, bsz),
        in_specs=[pl.BlockSpec((None, t_len, LANES), lambda j, b: (b, 0, j))] + [tab_spec(tabs[n]) for n in names],
        out_specs=pl.BlockSpec((None, t_len, LANES), lambda j, b: (b, 0, j)),
        out_shape=jax.ShapeDtypeStruct(u.shape, F32),
        compiler_params=pltpu.CompilerParams(vmem_limit_bytes=VMEM_LIMIT),
        name="s5_mixer",
    )(u, *[tabs[n] for n in names])


def _s5_tables(lam_re, lam_im, log_dt, b_re, b_im, c_re, c_im, d_skip, w_glu, b_glu, max_levels):
    hp = lax.Precision.HIGHEST
    L, G, P, H = S5_CHUNK, N_S5_GROUPS, S5_STATE, S5_GROUP
    dt = jnp.exp(log_dt)[:, :, None]
    are, aim = lam_re * dt, lam_im * dt

    def apow(n):
        n = jnp.asarray(n, F32)
        shape = n.shape + (1, 1, 1)
        mag = jnp.exp(n.reshape(shape) * are)
        ang = n.reshape(shape) * aim
        return mag * jnp.cos(ang), mag * jnp.sin(ang)

    a1r, a1i = apow(jnp.ones(()))
    den = lam_re * lam_re + lam_im * lam_im
    qr = ((a1r - 1.0) * lam_re + a1i * lam_im) / den
    qi = (a1i * lam_re - (a1r - 1.0) * lam_im) / den
    bbr = qr[..., None] * b_re - qi[..., None] * b_im
    bbi = qr[..., None] * b_im + qi[..., None] * b_re
    pr, pi = apow(jnp.arange(L + 1))

    abr = pr[..., None] * bbr - pi[..., None] * bbi
    abi = pr[..., None] * bbi + pi[..., None] * bbr
    s_idx = jnp.arange(L)
    wf_r = jnp.transpose(abr[L - 1 - s_idx, 0], (1, 0, 3, 2))
    wf_i = jnp.transpose(abi[L - 1 - s_idx, 0], (1, 0, 3, 2))
    wb_r = jnp.transpose(abr[s_idx, 1], (1, 0, 3, 2))
    wb_i = jnp.transpose(abi[s_idx, 1], (1, 0, 3, 2))
    kf = (jnp.einsum("ghp,ngpi->nghi", c_re[0], abr[:L, 0], precision=hp)
          - jnp.einsum("ghp,ngpi->nghi", c_im[0], abi[:L, 0], precision=hp))
    kb = (jnp.einsum("ghp,ngpi->nghi", c_re[1], abr[:L, 1], precision=hp)
          - jnp.einsum("ghp,ngpi->nghi", c_im[1], abi[:L, 1], precision=hp))
    lag = s_idx[None, :] - s_idx[:, None]
    t1 = s_idx + 1
    cf_r = (c_re[0][:, None] * pr[t1, 0].transpose(1, 0, 2)[:, :, None, :]
            - c_im[0][:, None] * pi[t1, 0].transpose(1, 0, 2)[:, :, None, :])
    cf_i = (c_re[0][:, None] * pi[t1, 0].transpose(1, 0, 2)[:, :, None, :]
            + c_im[0][:, None] * pr[t1, 0].transpose(1, 0, 2)[:, :, None, :])
    t2 = L - s_idx
    cb_r = (c_re[1][:, None] * pr[t2, 1].transpose(1, 0, 2)[:, :, None, :]
            - c_im[1][:, None] * pi[t2, 1].transpose(1, 0, 2)[:, :, None, :])
    cb_i = (c_re[1][:, None] * pi[t2, 1].transpose(1, 0, 2)[:, :, None, :]
            + c_im[1][:, None] * pr[t2, 1].transpose(1, 0, 2)[:, :, None, :])

    eye2 = jnp.eye(2, dtype=F32)
    g2 = G // 2

    def pair_block(m):
        m = m.reshape(g2, 2, H, H)
        m = m[:, :, :, None, :] * eye2[None, :, None, :, None]
        return m.reshape(g2, 2 * H, 2 * H)

    def chunk_sq(sel, blocks):
        acc = None
        for sel_n, blk_n in zip(sel, blocks):
            term = sel_n.astype(F32)[None, :, None, :, None] * blk_n[:, None, :, None, :]
            acc = term if acc is None else acc + term
        return acc.reshape(g2, L * 2 * H, L * 2 * H)

    def pair_in(m):
        m = m.reshape(g2, 2, L, H, P)
        m = m[:, :, :, :, None, :] * eye2[None, :, None, None, :, None]
        return jnp.transpose(m, (0, 2, 1, 3, 4, 5)).reshape(g2, L * 2 * H, 2 * P)

    def pair_out(m):
        m = jnp.transpose(m, (0, 3, 1, 2)).reshape(g2, 2, P, L, H)
        m = m[:, :, :, :, None, :] * eye2[None, :, None, None, :, None]
        return m.reshape(g2, 2 * P, L * 2 * H)

    wz = jnp.concatenate([pair_in(m) for m in (wf_r, wf_i, wb_r, wb_i)], axis=2)
    cz = jnp.concatenate([pair_out(m) for m in (cf_r, -cf_i, cb_r, -cb_i)], axis=1)
    taps = ([pair_block(jnp.swapaxes(kf[n], 1, 2)) for n in range(L)]
            + [pair_block(jnp.swapaxes(kb[n], 1, 2)) for n in range(L)])
    sels = [lag == n for n in range(L)] + [lag == -n for n in range(L)]
    tz = chunk_sq(sels, taps)
    wglu = chunk_sq([lag == 0], [pair_block(w_glu)])

    def lanes_gh(vec):
        return jnp.broadcast_to(vec.reshape(g2, 1, 2, H), (g2, L, 2, H)).reshape(g2, 1, L * 2 * H)

    lev_n = (L * (2 ** jnp.arange(max_levels))).astype(F32)
    lr, li = apow(lev_n)

    def lanes_gp(m):
        return jnp.transpose(m.reshape(max_levels, G // 2, 2 * P), (1, 0, 2))[:, :, None, :]

    pw = jnp.concatenate([lanes_gp(lr[:, 0]), lanes_gp(li[:, 0]), lanes_gp(lr[:, 1]), lanes_gp(li[:, 1])],
                         axis=2)
    return {
        "wz": wz.astype(BF16), "tz": tz.astype(BF16), "cz": cz.astype(BF16), "pw": pw,
        "dvec": lanes_gh(d_skip), "wglu": wglu.astype(BF16), "bglu": lanes_gh(b_glu),
    }


def _rms(x):
    return x * lax.rsqrt(jnp.mean(x * x, axis=-1, keepdims=True) + RMS_EPS)


def _out_kernel(x_ref, yr_ref, zs_ref, mod_ref, sg_ref, wo_ref, n2_ref, w1_ref, w3_ref, w2_ref, fg_ref, o_ref):
    gate1 = mod_ref[2:3, :]
    shift2 = mod_ref[3:4, :]
    scale2 = mod_ref[4:5, :]
    gate2 = mod_ref[5:6, :]
    ys = _rms(zs_ref[...]) * sg_ref[...]
    mix = (jnp.dot(yr_ref[...].astype(BF16), wo_ref[:D_RWKV, :], preferred_element_type=F32)
           + jnp.dot(ys.astype(BF16), wo_ref[D_RWKV:, :], preferred_element_type=F32))
    x1 = x_ref[...] + gate1 * mix
    h = ((_rms(x1) * n2_ref[...]) * (1.0 + scale2) + shift2).astype(BF16)
    f1 = jnp.dot(h, w1_ref[...], preferred_element_type=F32)
    f3 = jnp.dot(h, w3_ref[...], preferred_element_type=F32)
    act = (f1 * _sigmoid(f1)) * f3
    f = jnp.dot(act.astype(BF16), w2_ref[...], preferred_element_type=F32)
    x2 = x1 + gate2 * f
    o_ref[...] = _rms(x2) * fg_ref[...]


def _out_call(x, y_rwkv, z_s5, mod3, s5_out_g, w_out, norm2_g, w_ff1, w_ff3, w_ff2, final_g, tb):
    bsz, t_len, _ = x.shape

    def resident(shape):
        return pl.BlockSpec(shape, lambda b, i: (0, 0), pipeline_mode=pl.Buffered(1))

    return pl.pallas_call(
        _out_kernel,
        grid=(bsz, t_len // tb),
        in_specs=[
            pl.BlockSpec((None, tb, D_MODEL), lambda b, i: (b, i, 0)),
            pl.BlockSpec((None, tb, D_RWKV), lambda b, i: (b, i, 0)),
            pl.BlockSpec((None, tb, D_S5), lambda b, i: (b, i, 0)),
            pl.BlockSpec((None, 6, D_MODEL), lambda b, i: (b, 0, 0)),
            _const_spec((1, D_S5)),
            resident((D_MODEL, D_MODEL)),
            _const_spec((1, D_MODEL)),
            resident((D_MODEL, D_FF)),
            resident((D_MODEL, D_FF)),
            resident((D_FF, D_MODEL)),
            _const_spec((1, D_MODEL)),
        ],
        out_specs=pl.BlockSpec((None, tb, D_MODEL), lambda b, i: (b, i, 0)),
        out_shape=jax.ShapeDtypeStruct(x.shape, F32),
        compiler_params=pltpu.CompilerParams(vmem_limit_bytes=VMEM_LIMIT),
        name="out_proj_ffn",
    )(x, y_rwkv, z_s5, mod3, s5_out_g, w_out, norm2_g, w_ff1, w_ff3, w_ff2, final_g)


def _block_sizes(t_len):
    tb_in = min(512, t_len)
    tbs = min(256, t_len)
    tb_out = min(256, t_len)
    return tb_in, tbs, tb_out


def _pad_lora(w, n_in):
    z = jnp.zeros_like(w[0])
    return [jnp.concatenate([w[0], z], axis=0).astype(BF16), jnp.concatenate([z, w[1]], axis=0).astype(BF16)]


def _trunk(x, mod3, prm, s5_tabs):
    bsz, t_len, _ = x.shape
    tb_in, tbs, tb_out = _block_sizes(t_len)
    ps, u = _in_call(x, mod3, prm["norm1_g"], prm["w_in"], prm["mu_shift"], tb_in)
    y_b = _scan_call(ps, prm["vecs"], prm["mats"], None, direction=1, tbs=tbs)
    y_rwkv = _scan_call(ps, prm["vecs"], prm["mats"], y_b, direction=0, tbs=tbs)
    z_s5 = _s5_call(u, s5_tabs)
    return _out_call(x, y_rwkv, z_s5, mod3, prm["s5_out_g"], prm["w_out"], prm["norm2_g"],
                     prm["w_ff1"], prm["w_ff3"], prm["w_ff2"], prm["final_g"], tb_out)


def _prepare(norm1_g, w_in, mu_shift, w0, w2, a0, a2, g2, k_k, k_a, r_k, lnx_g, lnx_b, s5_out_g, w_out,
             norm2_g, w_ff1, w_ff3, w_ff2, final_g):
    row = lambda v: v.reshape(1, -1)
    vecs = {
        "w0": [row(w0[0]), row(w0[1])], "a0": [row(a0[0]), row(a0[1])],
        "k_k": row(k_k), "k_a": row(k_a), "r_k": row(r_k), "lnx_g": row(lnx_g), "lnx_b": row(lnx_b),
    }
    mats = {"w2p": _pad_lora(w2, 64), "a2p": _pad_lora(a2, 64), "g2": g2.astype(BF16)}
    return {
        "norm1_g": row(norm1_g), "w_in": w_in.astype(BF16), "mu_shift": row(mu_shift),
        "vecs": vecs, "mats": mats, "s5_out_g": row(s5_out_g), "w_out": w_out.astype(BF16),
        "norm2_g": row(norm2_g), "w_ff1": w_ff1.astype(BF16), "w_ff3": w_ff3.astype(BF16),
        "w_ff2": w_ff2.astype(BF16), "final_g": row(final_g),
    }


def kernel(x_prompt, x_sample, c_prompt, c_sample, norm1_g, w_ada, b_ada, w_in, mu_shift, w0, w2, a0, a2, g2,
           k_k, k_a, r_k, lnx_g, lnx_b, lam_re, lam_im, log_dt, b_re, b_im, c_re, c_im, d_skip, w_glu, b_glu,
           s5_out_g, w_out, norm2_g, w_ff1, w_ff3, w_ff2, final_g):
    depth = norm1_g.shape[0]
    assert depth == 1, "the fused output kernel applies the final norm, so it supports a single layer"
    nbp = x_prompt.shape[0]
    xs = [x_prompt, x_sample]
    c_all = jnp.concatenate([c_prompt, c_sample], axis=0)
    max_t = max(x_prompt.shape[1], x_sample.shape[1])
    max_levels = max(1, (max_t // S5_CHUNK - 1).bit_length())
    for i in range(depth):
        prm = _prepare(norm1_g[i], w_in[i], mu_shift[i], w0[i], w2[i], a0[i], a2[i], g2[i], k_k[i], k_a[i],
                       r_k[i], lnx_g[i], lnx_b[i], s5_out_g[i], w_out[i], norm2_g[i], w_ff1[i], w_ff3[i],
                       w_ff2[i], final_g)
        s5_tabs = _s5_tables(lam_re[i], lam_im[i], log_dt[i], b_re[i], b_im[i], c_re[i], c_im[i],
                             d_skip[i], w_glu[i], b_glu[i], max_levels)
        mod = _mod_call(c_all, w_ada[i].astype(BF16), b_ada[i].reshape(1, -1))
        mod3 = mod.reshape(c_all.shape[0], 6, D_MODEL)
        xs = [_trunk(xs[0], mod3[:nbp], prm, s5_tabs), _trunk(xs[1], mod3[nbp:], prm, s5_tabs)]
    return tuple(xs)
```

```python
import functools
import math

import jax
import jax.numpy as jnp
import numpy as np
from jax import lax
from jax.experimental import pallas as pl
from jax.experimental.pallas import tpu as pltpu

F32 = jnp.float32
BF16 = jnp.bfloat16

D_MODEL = 1024
D_RWKV = 512
HEAD_DIM = 64
D_S5 = 512
S5_GROUP = 16
N_S5_GROUPS = 32
S5_STATE = 64
D_FF = 2816
RWKV_COLS = 1920
D_IN_PROJ = 2432
RMS_EPS = 1e-6
LNX_EPS = 64e-5

LANES = 128
CHUNK = 64
S5_CHUNK = 16
N_PAIRS = D_RWKV // LANES
N_S5_PAIRS = N_S5_GROUPS // 2
VMEM_LIMIT = 56 * 1024 * 1024


def _dot(a, b):
    return jnp.dot(a.astype(BF16), b.astype(BF16), preferred_element_type=F32)


def _split2(x):
    hi = x.astype(BF16)
    lo = (x - hi.astype(F32)).astype(BF16)
    return hi, lo


def _split3(x):
    hi = x.astype(BF16)
    r1 = x - hi.astype(F32)
    mid = r1.astype(BF16)
    lo = (r1 - mid.astype(F32)).astype(BF16)
    return hi, mid, lo


def _dot_exact_lhs(m, x):
    mh = m.astype(BF16)
    h, mid, lo = _split3(x)
    d = functools.partial(jnp.dot, preferred_element_type=F32)
    return d(mh, h) + (d(mh, mid) + d(mh, lo))


def _sigmoid(x):
    return 1.0 / (1.0 + jnp.exp(-x))


def _softplus(x):
    return jnp.maximum(x, 0.0) + jnp.log(1.0 + jnp.exp(-jnp.abs(x)))


def _const_spec(shape):
    nd = len(shape)
    return pl.BlockSpec(shape, lambda *_: (0,) * nd)


def _mod_kernel(c_ref, w_ref, b_ref, o_ref):
    c = c_ref[...]
    s = c * _sigmoid(c)
    o_ref[...] = _dot(s, w_ref[...]) + b_ref[...]


def _mod_call(c_all, w_ada, b_ada):
    nb = c_all.shape[0]
    ncol = w_ada.shape[1] // D_MODEL
    return pl.pallas_call(
        _mod_kernel,
        grid=(ncol,),
        in_specs=[
            pl.BlockSpec((nb, D_MODEL), lambda j: (0, 0)),
            pl.BlockSpec((D_MODEL, D_MODEL), lambda j: (0, j)),
            pl.BlockSpec((1, D_MODEL), lambda j: (0, j)),
        ],
        out_specs=pl.BlockSpec((nb, D_MODEL), lambda j: (0, j)),
        out_shape=jax.ShapeDtypeStruct((nb, w_ada.shape[1]), F32),
        name="adaln_mod",
    )(c_all, w_ada, b_ada)


HALO = 8


def _in_kernel(x_ref, xp_ref, xn_ref, mod_ref, g_ref, w_ref, mu_ref, ps_ref, u_ref, p_scr, *, tb, nt):
    i = pl.program_id(1)
    shift = mod_ref[0:1, :]
    scale = mod_ref[1:2, :]
    g = g_ref[...]
    xa = jnp.concatenate([xp_ref[...], x_ref[...], xn_ref[...]], axis=0)
    ms = jnp.mean(xa * xa, axis=-1, keepdims=True)
    h = ((xa * lax.rsqrt(ms + RMS_EPS)) * g) * (1.0 + scale) + shift
    hb = h.astype(BF16)
    p_scr[...] = jnp.dot(hb, w_ref[:, :RWKV_COLS], preferred_element_type=F32)
    u_ref[...] = jnp.dot(hb, w_ref[:, RWKV_COLS:], preferred_element_type=F32)[HALO:HALO + tb]
    p = p_scr[HALO:HALO + tb, :]
    prev = p_scr[HALO - 1:HALO - 1 + tb, :]
    nxt = p_scr[HALO + 1:HALO + 1 + tb, :]
    row = lax.broadcasted_iota(jnp.int32, (tb, 1), 0)
    prev = jnp.where(jnp.logical_and(row == 0, i == 0), 0.0, prev)
    nxt = jnp.where(jnp.logical_and(row == tb - 1, i == nt - 1), 0.0, nxt)
    ps_ref[...] = p + mu_ref[...] * (0.5 * (prev + nxt) - p)


def _in_call(x, mod3, norm1_g, w_in, mu_shift, tb):
    bsz, t_len, _ = x.shape
    nt = t_len // tb
    hb = tb // HALO
    last = t_len // HALO - 1
    kern = functools.partial(_in_kernel, tb=tb, nt=nt)
    return pl.pallas_call(
        kern,
        grid=(bsz, nt),
        in_specs=[
            pl.BlockSpec((None, tb, D_MODEL), lambda b, i: (b, i, 0)),
            pl.BlockSpec((None, HALO, D_MODEL), lambda b, i: (b, jnp.maximum(i * hb - 1, 0), 0)),
            pl.BlockSpec((None, HALO, D_MODEL), lambda b, i: (b, jnp.minimum((i + 1) * hb, last), 0)),
            pl.BlockSpec((None, 6, D_MODEL), lambda b, i: (b, 0, 0)),
            _const_spec((1, D_MODEL)),
            pl.BlockSpec((D_MODEL, D_IN_PROJ), lambda b, i: (0, 0), pipeline_mode=pl.Buffered(1)),
            _const_spec((1, RWKV_COLS)),
        ],
        out_specs=[
            pl.BlockSpec((None, tb, RWKV_COLS), lambda b, i: (b, i, 0)),
            pl.BlockSpec((None, tb, D_S5), lambda b, i: (b, i, 0)),
        ],
        out_shape=[
            jax.ShapeDtypeStruct((bsz, t_len, RWKV_COLS), F32),
            jax.ShapeDtypeStruct((bsz, t_len, D_S5), F32),
        ],
        scratch_shapes=[pltpu.VMEM((tb + 2 * HALO, RWKV_COLS), F32)],
        compiler_params=pltpu.CompilerParams(vmem_limit_bytes=VMEM_LIMIT),
        name="in_proj_shift",
    )(x, x, x, mod3, norm1_g, w_in, mu_shift)


def _head_sum(x, blk):
    h, lo = _split2(x)
    d = functools.partial(jnp.dot, preferred_element_type=F32)
    return d(h, blk) + d(lo, blk)


def _bdot(a, b):
    return jnp.dot(a, b, preferred_element_type=F32)


def _side_by_side_diag(x):
    z = jnp.zeros((x.shape[0], LANES), x.dtype)
    return jnp.concatenate([jnp.concatenate([x[:, :LANES], z], axis=1),
                            jnp.concatenate([z, x[:, LANES:]], axis=1)], axis=0)


def _scan_masks(tbs, reverse):
    two = 2 * CHUNK
    ri = np.arange(two)[:, None]
    ci = np.arange(2 * LANES)[None, :] & (LANES - 1)
    before = (ci > ri) if reverse else (ci < ri)
    levels = [((ri >> 1) == (ci >> 1)) & before]
    sh = 1
    while (1 << sh) < CHUNK:
        levels.append(((ri >> (sh + 1)) == (ci >> (sh + 1))) & ((ri >> sh) != (ci >> sh)) & before)
        sh += 1
    rt, ct = np.arange(tbs)[:, None], np.arange(tbs)[None, :]
    tri = ((rt >> 6) == (ct >> 6)) & ((ct >= rt) if reverse else (ct <= rt))
    hb = np.arange(D_RWKV)
    blk = (hb[:, None] >> 6) == (hb[None, :] >> 6)
    return {
        "eye": jnp.asarray((ri == ci).astype(np.float32)),
        "levels": jnp.asarray(np.stack(levels).astype(np.float32), dtype=BF16),
        "tri": jnp.asarray(tri.astype(np.float32), dtype=BF16),
        "blk": jnp.asarray(blk.astype(np.float32), dtype=BF16),
    }


def _tri_inverse_batch(n_raw, eye, levels_ref, reverse):
    two = 2 * CHUNK
    n_sbs = [jnp.concatenate([n_raw[i], n_raw[i + 1]], axis=1) for i in range(0, len(n_raw), 2)]
    n_bf = [n.astype(BF16) for n in n_sbs]
    m1 = levels_ref[0]
    t = [eye + (n * m1).astype(F32) for n in n_bf]
    sh = 1
    while (1 << sh) < CHUNK:
        mk = levels_ref[sh]
        cm = [_side_by_side_diag(n * mk) for n in n_bf]
        tb = [x.astype(BF16) for x in t]
        half = 1 << sh
        if half % 8 == 0:
            off = 0 if reverse else half
            starts = [b0 + off for b0 in range(0, two, 2 * half)]
            pick = lambda x: jnp.concatenate([x[s0:s0 + half] for s0 in starts], axis=0)
            xs = [_bdot(pick(a), c).astype(BF16) for a, c in zip(tb, cm)]
            upd = [_bdot(a, _side_by_side_diag(b)) for a, b in zip(xs, tb)]
            new_t = []
            for x, u in zip(t, upd):
                parts = []
                for i, s0 in enumerate(starts):
                    changed = x[s0:s0 + half] + u[i * half:(i + 1) * half]
                    keep = x[s0 - off + (half - off):s0 - off + (half - off) + half]
                    parts += [changed, keep] if reverse else [keep, changed]
                new_t.append(jnp.concatenate(parts, axis=0))
            t = new_t
        else:
            xs = [_bdot(a, c).astype(BF16) for a, c in zip(tb, cm)]
            t = [x + _bdot(a, _side_by_side_diag(b)) for x, a, b in zip(t, xs, tb)]
        sh += 1
    out = []
    for t0 in t:
        out += [t0[:, :LANES], t0[:, LANES:]]
    return out


def _scan_kernel(*refs, tbs, reverse, finalize):
    if finalize:
        (r_ref, k_ref, v_ref, lora_ref, w0_ref, w2_ref, a0_ref, a2_ref, kk_ref, ka_ref,
         eye_ref, levels_ref, tri_ref, blk_ref,
         rk_ref, g2_ref, lg_ref, lb_ref, a0o_ref, a2o_ref, yo_ref, y_ref, st_scr) = refs
    else:
        (r_ref, k_ref, v_ref, lora_ref, w0_ref, w2_ref, a0_ref, a2_ref, kk_ref, ka_ref,
         eye_ref, levels_ref, tri_ref, blk_ref, y_ref, st_scr) = refs

    @pl.when(pl.program_id(1) == 0)
    def _():
        st_scr[...] = jnp.zeros_like(st_scr)

    two = 2 * CHUNK
    ri = lax.broadcasted_iota(jnp.int32, (two, two), 0)
    ci = lax.broadcasted_iota(jnp.int32, (two, two), 1)
    same_head = (ri >> 6) == (ci >> 6)
    before = (ci > ri) if reverse else (ci < ri)
    bd_strict = jnp.logical_and(same_head, before)
    bd_incl = jnp.logical_and(same_head, jnp.logical_or(before, ri == ci))
    lane = lax.broadcasted_iota(jnp.int32, (CHUNK, LANES), 1)
    head0 = lane < HEAD_DIM
    tri = tri_ref[...]
    blk = blk_ref[...]

    def stack(x):
        return jnp.concatenate([jnp.where(head0, x, 0.0), jnp.where(head0, 0.0, x)], axis=0)

    k = k_ref[...]
    xa = lora_ref[:, LANES:2 * LANES]
    zw = w0_ref[...] + _dot(jnp.tanh(lora_ref[:, :LANES]), w2_ref[...])
    wd = -_softplus(-zw) - 0.5
    lw = -jnp.exp(wd)
    a = _sigmoid(a0_ref[...] + _dot(xa, a2_ref[...]))
    ka = ka_ref[...]
    kd = k * (1.0 + (a - 1.0) * ka)
    kkr = k * kk_ref[...]
    kkn = kkr * lax.rsqrt(jnp.maximum(_head_sum(kkr * kkr, blk), 1e-24))

    n_chunks = tbs // CHUNK
    units = [(ch, q) for ch in range(n_chunks) for q in range(N_PAIRS)]

    c_blk = _dot_exact_lhs(tri, lw)
    r_blk, v_blk = r_ref[...], v_ref[...]
    pre, n_raw = {}, []
    for ch, q in units:
        rows = slice(ch * CHUNK, (ch + 1) * CHUNK)
        ls = slice(q * LANES, (q + 1) * LANES)
        c, lwc = c_blk[rows, ls], lw[rows, ls]
        ctot = c[0:1, :] if reverse else c[CHUNK - 1:CHUNK, :]
        eg = jnp.exp(c)
        ei = jnp.exp(-c)
        ex = jnp.exp(c - lwc)
        ee = jnp.exp(ctot - c)
        akk = a[rows, ls] * kkn[rows, ls]
        kdc = kd[rows, ls]
        rt = r_blk[rows, ls] * eg
        at = -(kkn[rows, ls] * ex)
        g = lax.dot_general(jnp.concatenate([at, rt], axis=0).astype(BF16),
                            jnp.concatenate([stack(akk * ei), stack(kdc * ei)], axis=0).astype(BF16),
                            (((1,), (1,)), ((), ())), preferred_element_type=F32)
        n_raw.append(stack(g[:CHUNK, :two]))
        pre[ch, q] = dict(
            rt_s=stack(rt).astype(BF16), at_s=stack(at).astype(BF16), v_s=stack(v_blk[rows, ls]).astype(BF16),
            aak=jnp.where(bd_strict, stack(g[:CHUNK, two:]), 0.0).astype(BF16),
            ark=jnp.where(bd_incl, stack(g[CHUNK:, two:]), 0.0).astype(BF16),
            arb=jnp.where(bd_incl, stack(g[CHUNK:, :two]), 0.0).astype(BF16),
            btet=jnp.transpose(stack(akk * ee)).astype(BF16),
            ktet=jnp.transpose(stack(kdc * ee)).astype(BF16),
            gcol=jnp.transpose(jnp.broadcast_to(jnp.exp(ctot), (two, LANES))),
        )
    t_inv = _tri_inverse_batch(n_raw, eye_ref[...], levels_ref, reverse)

    def sbs(ch, m, name):
        return jnp.concatenate([pre[ch, 2 * m][name], pre[ch, 2 * m + 1][name]], axis=1)

    half = N_PAIRS // 2
    duo = {}
    for ch in range(n_chunks):
        for m in range(half):
            lhs = jnp.concatenate([sbs(ch, m, "aak"), sbs(ch, m, "ark"), sbs(ch, m, "ktet")], axis=0)
            avk = _bdot(lhs, _side_by_side_diag(sbs(ch, m, "v_s")))
            wms, ulocs = [], []
            for j in range(2):
                q = 2 * m + j
                av = avk[:two, j * LANES:(j + 1) * LANES].astype(BF16)
                rhs = jnp.concatenate([pre[ch, q]["at_s"], av], axis=1)
                wu = _bdot(t_inv[ch * N_PAIRS + q].astype(BF16), rhs)
                wms.append(wu[:, :LANES].astype(BF16))
                ulocs.append(wu[:, LANES:])
            duo[ch, m] = dict(
                lhs_p=jnp.concatenate([jnp.concatenate(wms, axis=1), sbs(ch, m, "rt_s")], axis=0),
                lhs_u=jnp.concatenate([sbs(ch, m, "arb"), sbs(ch, m, "btet")], axis=0),
                uloc=jnp.concatenate(ulocs, axis=1), yloc=avk[two:2 * two], kvt=avk[2 * two:],
                gcol=sbs(ch, m, "gcol"),
            )

    st = [st_scr[m] for m in range(half)]
    order = range(n_chunks - 1, -1, -1) if reverse else range(n_chunks)
    for ch in order:
        ys = []
        for m in range(half):
            d = duo[ch, m]
            p = _bdot(d["lhs_p"], _side_by_side_diag(st[m].astype(BF16)))
            u = _side_by_side_diag((p[:two] + d["uloc"]).astype(BF16))
            ab = _bdot(d["lhs_u"], u)
            y_s = p[two:] + ab[:two] + d["yloc"]
            ys.append(y_s[:CHUNK] + y_s[CHUNK:])
            st[m] = st[m] * d["gcol"] + ab[two:] + d["kvt"]
        y_ref[ch * CHUNK:(ch + 1) * CHUNK, :] = jnp.concatenate(ys, axis=1)
    for m in range(half):
        st_scr[m] = st[m]

    if finalize:
        y = y_ref[...] + yo_ref[...]
        inv_n = 1.0 / HEAD_DIM
        mean = _head_sum(y, blk) * inv_n
        d = y - mean
        var = _head_sum(d * d, blk) * inv_n
        yn = d * lax.rsqrt(var + LNX_EPS) * lg_ref[...] + lb_ref[...]
        a_o = _sigmoid(a0o_ref[...] + _dot(xa, a2o_ref[...]))
        kd_o = k * (1.0 + (a_o - 1.0) * ka)
        rr = r_ref[...] * rk_ref[...]
        bonus = (_head_sum(rr * kd, blk) + _head_sum(rr * kd_o, blk)) * v_ref[...]
        gate = _dot(_sigmoid(lora_ref[:, 2 * LANES:]), g2_ref[...])
        y_ref[...] = (yn + bonus) * gate


def _scan_call(ps, vecs, mats, y_other, *, direction, tbs):
    bsz, t_len, _ = ps.shape
    nt = t_len // tbs
    reverse = direction == 1
    finalize = y_other is not None
    lora_w = RWKV_COLS - 3 * D_RWKV

    def tmap(i):
        return (nt - 1 - i) if reverse else i

    def col_spec(col):
        return pl.BlockSpec((None, tbs, D_RWKV), lambda b, i: (b, tmap(i), col))

    vec_spec = _const_spec((1, D_RWKV))
    mat_spec = _const_spec((LANES, D_RWKV))
    in_specs = [
        col_spec(0), col_spec(1), col_spec(2),
        pl.BlockSpec((None, tbs, lora_w), lambda b, i: (b, tmap(i), 3 * D_RWKV // lora_w)),
        vec_spec, mat_spec, vec_spec, mat_spec, vec_spec, vec_spec,
    ]
    o = 1 - direction
    args = [ps] * 4 + [vecs["w0"][direction], mats["w2p"][direction], vecs["a0"][direction],
                       mats["a2p"][direction], vecs["k_k"], vecs["k_a"]]
    masks = _scan_masks(tbs, reverse)
    for name in ("eye", "levels", "tri", "blk"):
        in_specs.append(_const_spec(masks[name].shape))
        args.append(masks[name])
    if finalize:
        in_specs += [vec_spec, mat_spec, vec_spec, vec_spec, vec_spec, mat_spec, col_spec(0)]
        args += [vecs["r_k"], mats["g2"], vecs["lnx_g"], vecs["lnx_b"], vecs["a0"][o], mats["a2p"][o], y_other]
    kern = functools.partial(_scan_kernel, tbs=tbs, reverse=reverse, finalize=finalize)
    scratch = [pltpu.VMEM((N_PAIRS // 2, 2 * CHUNK, 2 * LANES), F32)]
    return pl.pallas_call(
        kern,
        grid=(bsz, nt),
        in_specs=in_specs,
        out_specs=col_spec(0),
        out_shape=jax.ShapeDtypeStruct((bsz, t_len, D_RWKV), F32),
        scratch_shapes=scratch,
        compiler_params=pltpu.CompilerParams(
            dimension_semantics=("arbitrary", "arbitrary"), vmem_limit_bytes=VMEM_LIMIT,
        ),
        name="rwkv7_scan_bwd" if reverse else "rwkv7_scan_fwd",
    )(*args)


S5_PAIR_LANES = 2 * S5_GROUP
S5_PAIRS_PER_STEP = LANES // S5_PAIR_LANES
S5_WIDTH = S5_CHUNK * S5_PAIR_LANES
S5_TOK_PER_VREG = LANES // S5_PAIR_LANES


def _gelu_tanh(x):
    return 0.5 * x * (1.0 + jnp.tanh(math.sqrt(2.0 / math.pi) * (x + 0.044715 * (x * x * x))))


def _s5_pair(u, wz, tz, cz, pw_ref, pj, dvec, wg, bg, nc, nlev):
    ub = u.astype(BF16)
    z = jnp.dot(ub, wz, preferred_element_type=F32)
    row = lax.broadcasted_iota(jnp.int32, (nc, LANES), 0)
    xs = [z[:, j * LANES:(j + 1) * LANES] for j in range(4)]

    def shifted(x, s, up):
        if up:
            return jnp.where(row >= nc - s, 0.0, pltpu.roll(x, nc - s, 0))
        return jnp.where(row < s, 0.0, pltpu.roll(x, s, 0))

    for lev in range(nlev):
        s = 1 << lev
        for d in range(2):
            xr, xi = xs[2 * d], xs[2 * d + 1]
            ar = pw_ref[pj, lev, 2 * d:2 * d + 1, :]
            ai = pw_ref[pj, lev, 2 * d + 1:2 * d + 2, :]
            pr = shifted(xr, s, d == 1)
            pi = shifted(xi, s, d == 1)
            xs[2 * d] = xr + (ar * pr - ai * pi)
            xs[2 * d + 1] = xi + (ar * pi + ai * pr)
    xin = jnp.concatenate([shifted(xs[0], 1, False), shifted(xs[1], 1, False),
                           shifted(xs[2], 1, True), shifted(xs[3], 1, True)], axis=1)
    y = (jnp.dot(ub, tz, preferred_element_type=F32)
         + jnp.dot(xin.astype(BF16), cz, preferred_element_type=F32)
         + dvec * u)
    zact = _gelu_tanh(y)
    gate = _sigmoid(jnp.dot(zact.astype(BF16), wg, preferred_element_type=F32) + bg)
    return zact * gate


def _s5_kernel(u_ref, wz_ref, tz_ref, cz_ref, pw_ref, d_ref, wg_ref, bg_ref, o_ref, *, nc, nlev):
    lane = lax.broadcasted_iota(jnp.int32, (nc, LANES), 1)
    seg = lane // S5_PAIR_LANES
    tok = [u_ref[pl.ds(s, nc, stride=S5_CHUNK), :] for s in range(S5_CHUNK)]
    outs = []
    for pj in range(S5_PAIRS_PER_STEP):
        cols = []
        for qv in range(S5_CHUNK // S5_TOK_PER_VREG):
            acc = None
            for i in range(S5_TOK_PER_VREG):
                x = tok[S5_TOK_PER_VREG * qv + i]
                sh = (S5_PAIR_LANES * (i - pj)) % LANES
                if sh:
                    x = pltpu.roll(x, sh, 1)
                acc = x if acc is None else jnp.where(seg == i, x, acc)
            cols.append(acc)
        u = jnp.concatenate(cols, axis=1)
        outs.append(_s5_pair(u, wz_ref[pj], tz_ref[pj], cz_ref[pj], pw_ref, pj, d_ref[pj], wg_ref[pj],
                             bg_ref[pj], nc, nlev))
    for s in range(S5_CHUNK):
        qv, i = divmod(s, S5_TOK_PER_VREG)
        acc = None
        for pj in range(S5_PAIRS_PER_STEP):
            x = outs[pj][:, qv * LANES:(qv + 1) * LANES]
            sh = (S5_PAIR_LANES * (pj - i)) % LANES
            if sh:
                x = pltpu.roll(x, sh, 1)
            acc = x if acc is None else jnp.where(seg == pj, x, acc)
        o_ref[pl.ds(s, nc, stride=S5_CHUNK), :] = acc


def _s5_call(u, tabs):
    bsz, t_len, width = u.shape
    nc = t_len // S5_CHUNK
    nsteps = width // LANES
    nlev = max(1, (nc - 1).bit_length())
    kern = functools.partial(_s5_kernel, nc=nc, nlev=nlev)
    pp = S5_PAIRS_PER_STEP

    def tab_spec(arr):
        nd = arr.ndim - 1
        return pl.BlockSpec((pp,) + arr.shape[1:], lambda j, b: (j,) + (0,) * nd)

    names = ["wz", "tz", "cz", "pw", "dvec", "wglu", "bglu"]
    return pl.pallas_call(
        kern,
        grid=(nsteps, bsz),
        in_specs=[pl.BlockSpec((None, t_len, LANES), lambda j, b: (b, 0, j))] + [tab_spec(tabs[n]) for n in names],
        out_specs=pl.BlockSpec((None, t_len, LANES), lambda j, b: (b, 0, j)),
        out_shape=jax.ShapeDtypeStruct(u.shape, F32),
        compiler_params=pltpu.CompilerParams(vmem_limit_bytes=VMEM_LIMIT),
        name="s5_mixer",
    )(u, *[tabs[n] for n in names])


def _toeplitz_kernel(strip_ref, o_ref):
    strip = strip_ref[...]
    for s in range(S5_CHUNK):
        start = S5_PAIR_LANES * (S5_CHUNK - 1 - s)
        o_ref[s * S5_PAIR_LANES:(s + 1) * S5_PAIR_LANES, :] = strip[:, start:start + S5_WIDTH].astype(BF16)


def _toeplitz_call(strips):
    npair, rows, width = strips.shape
    return pl.pallas_call(
        _toeplitz_kernel,
        grid=(npair,),
        in_specs=[pl.BlockSpec((None, rows, width), lambda p: (p, 0, 0))],
        out_specs=pl.BlockSpec((None, S5_WIDTH, S5_WIDTH), lambda p: (p, 0, 0)),
        out_shape=jax.ShapeDtypeStruct((npair, S5_WIDTH, S5_WIDTH), BF16),
        name="s5_toeplitz_expand",
    )(strips)


def _s5_tables(lam_re, lam_im, log_dt, b_re, b_im, c_re, c_im, d_skip, w_glu, b_glu, max_levels):
    hp = lax.Precision.HIGHEST
    L, G, P, H = S5_CHUNK, N_S5_GROUPS, S5_STATE, S5_GROUP
    dt = jnp.exp(log_dt)[:, :, None]
    are, aim = lam_re * dt, lam_im * dt

    def apow(n):
        n = jnp.asarray(n, F32)
        shape = n.shape + (1, 1, 1)
        mag = jnp.exp(n.reshape(shape) * are)
        ang = n.reshape(shape) * aim
        return mag * jnp.cos(ang), mag * jnp.sin(ang)

    a1r, a1i = apow(jnp.ones(()))
    den = lam_re * lam_re + lam_im * lam_im
    qr = ((a1r - 1.0) * lam_re + a1i * lam_im) / den
    qi = (a1i * lam_re - (a1r - 1.0) * lam_im) / den
    bbr = qr[..., None] * b_re - qi[..., None] * b_im
    bbi = qr[..., None] * b_im + qi[..., None] * b_re
    pr, pi = apow(jnp.arange(L + 1))

    abr = pr[..., None] * bbr - pi[..., None] * bbi
    abi = pr[..., None] * bbi + pi[..., None] * bbr
    s_idx = jnp.arange(L)
    wf_r = jnp.transpose(abr[L - 1 - s_idx, 0], (1, 0, 3, 2))
    wf_i = jnp.transpose(abi[L - 1 - s_idx, 0], (1, 0, 3, 2))
    wb_r = jnp.transpose(abr[s_idx, 1], (1, 0, 3, 2))
    wb_i = jnp.transpose(abi[s_idx, 1], (1, 0, 3, 2))
    kf = (jnp.einsum("ghp,ngpi->nghi", c_re[0], abr[:L, 0], precision=hp)
          - jnp.einsum("ghp,ngpi->nghi", c_im[0], abi[:L, 0], precision=hp))
    kb = (jnp.einsum("ghp,ngpi->nghi", c_re[1], abr[:L, 1], precision=hp)
          - jnp.einsum("ghp,ngpi->nghi", c_im[1], abi[:L, 1], precision=hp))
    lag = s_idx[None, :] - s_idx[:, None]
    t1 = s_idx + 1
    cf_r = (c_re[0][:, None] * pr[t1, 0].transpose(1, 0, 2)[:, :, None, :]
            - c_im[0][:, None] * pi[t1, 0].transpose(1, 0, 2)[:, :, None, :])
    cf_i = (c_re[0][:, None] * pi[t1, 0].transpose(1, 0, 2)[:, :, None, :]
            + c_im[0][:, None] * pr[t1, 0].transpose(1, 0, 2)[:, :, None, :])
    t2 = L - s_idx
    cb_r = (c_re[1][:, None] * pr[t2, 1].transpose(1, 0, 2)[:, :, None, :]
            - c_im[1][:, None] * pi[t2, 1].transpose(1, 0, 2)[:, :, None, :])
    cb_i = (c_re[1][:, None] * pi[t2, 1].transpose(1, 0, 2)[:, :, None, :]
            + c_im[1][:, None] * pr[t2, 1].transpose(1, 0, 2)[:, :, None, :])

    eye2 = jnp.eye(2, dtype=F32)
    g2 = G // 2

    def pair_block(m):
        m = m.reshape(g2, 2, H, H)
        m = m[:, :, :, None, :] * eye2[None, :, None, :, None]
        return m.reshape(g2, 2 * H, 2 * H)

    def tap_strip(neg, pos):
        taps = [neg[n] for n in range(L - 1, 0, -1)] + [neg[0] + pos[0]] + [pos[n] for n in range(1, L)]
        return jnp.concatenate(taps, axis=2)

    def pair_in(m):
        m = m.reshape(g2, 2, L, H, P)
        m = m[:, :, :, :, None, :] * eye2[None, :, None, None, :, None]
        return jnp.transpose(m, (0, 2, 1, 3, 4, 5)).reshape(g2, L * 2 * H, 2 * P)

    def pair_out(m):
        m = jnp.transpose(m, (0, 3, 1, 2)).reshape(g2, 2, P, L, H)
        m = m[:, :, :, :, None, :] * eye2[None, :, None, None, :, None]
        return m.reshape(g2, 2 * P, L * 2 * H)

    wz = jnp.concatenate([pair_in(m) for m in (wf_r, wf_i, wb_r, wb_i)], axis=2)
    cz = jnp.concatenate([pair_out(m) for m in (cf_r, -cf_i, cb_r, -cb_i)], axis=1)
    fwd_taps = [pair_block(jnp.swapaxes(kf[n], 1, 2)) for n in range(L)]
    bwd_taps = [pair_block(jnp.swapaxes(kb[n], 1, 2)) for n in range(L)]
    tz = _toeplitz_call(tap_strip(bwd_taps, fwd_taps))
    zero = jnp.zeros((g2, 2 * H, 2 * H), F32)
    wglu = _toeplitz_call(tap_strip([zero] * L, [pair_block(w_glu)] + [zero] * (L - 1)))

    def lanes_gh(vec):
        return jnp.broadcast_to(vec.reshape(g2, 1, 2, H), (g2, L, 2, H)).reshape(g2, 1, L * 2 * H)

    lev_n = (L * (2 ** jnp.arange(max_levels))).astype(F32)
    lr, li = apow(lev_n)

    def lanes_gp(m):
        return jnp.transpose(m.reshape(max_levels, G // 2, 2 * P), (1, 0, 2))[:, :, None, :]

    pw = jnp.concatenate([lanes_gp(lr[:, 0]), lanes_gp(li[:, 0]), lanes_gp(lr[:, 1]), lanes_gp(li[:, 1])],
                         axis=2)
    return {
        "wz": wz.astype(BF16), "tz": tz.astype(BF16), "cz": cz.astype(BF16), "pw": pw,
        "dvec": lanes_gh(d_skip), "wglu": wglu.astype(BF16), "bglu": lanes_gh(b_glu),
    }


def _rms(x):
    return x * lax.rsqrt(jnp.mean(x * x, axis=-1, keepdims=True) + RMS_EPS)


def _out_kernel(x_ref, yr_ref, zs_ref, mod_ref, sg_ref, wo_ref, n2_ref, w1_ref, w3_ref, w2_ref, fg_ref, o_ref):
    gate1 = mod_ref[2:3, :]
    shift2 = mod_ref[3:4, :]
    scale2 = mod_ref[4:5, :]
    gate2 = mod_ref[5:6, :]
    ys = _rms(zs_ref[...]) * sg_ref[...]
    mix = (jnp.dot(yr_ref[...].astype(BF16), wo_ref[:D_RWKV, :], preferred_element_type=F32)
           + jnp.dot(ys.astype(BF16), wo_ref[D_RWKV:, :], preferred_element_type=F32))
    x1 = x_ref[...] + gate1 * mix
    h = ((_rms(x1) * n2_ref[...]) * (1.0 + scale2) + shift2).astype(BF16)
    f1 = jnp.dot(h, w1_ref[...], preferred_element_type=F32)
    f3 = jnp.dot(h, w3_ref[...], preferred_element_type=F32)
    act = (f1 * _sigmoid(f1)) * f3
    f = jnp.dot(act.astype(BF16), w2_ref[...], preferred_element_type=F32)
    x2 = x1 + gate2 * f
    o_ref[...] = _rms(x2) * fg_ref[...]


def _out_call(x, y_rwkv, z_s5, mod3, s5_out_g, w_out, norm2_g, w_ff1, w_ff3, w_ff2, final_g, tb):
    bsz, t_len, _ = x.shape

    def resident(shape):
        return pl.BlockSpec(shape, lambda b, i: (0, 0), pipeline_mode=pl.Buffered(1))

    return pl.pallas_call(
        _out_kernel,
        grid=(bsz, t_len // tb),
        in_specs=[
            pl.BlockSpec((None, tb, D_MODEL), lambda b, i: (b, i, 0)),
            pl.BlockSpec((None, tb, D_RWKV), lambda b, i: (b, i, 0)),
            pl.BlockSpec((None, tb, D_S5), lambda b, i: (b, i, 0)),
            pl.BlockSpec((None, 6, D_MODEL), lambda b, i: (b, 0, 0)),
            _const_spec((1, D_S5)),
            resident((D_MODEL, D_MODEL)),
            _const_spec((1, D_MODEL)),
            resident((D_MODEL, D_FF)),
            resident((D_MODEL, D_FF)),
            resident((D_FF, D_MODEL)),
            _const_spec((1, D_MODEL)),
        ],
        out_specs=pl.BlockSpec((None, tb, D_MODEL), lambda b, i: (b, i, 0)),
        out_shape=jax.ShapeDtypeStruct(x.shape, F32),
        compiler_params=pltpu.CompilerParams(vmem_limit_bytes=VMEM_LIMIT),
        name="out_proj_ffn",
    )(x, y_rwkv, z_s5, mod3, s5_out_g, w_out, norm2_g, w_ff1, w_ff3, w_ff2, final_g)


def _block_sizes(t_len):
    tb_in = min(512, t_len)
    tbs = min(256, t_len)
    tb_out = min(256, t_len)
    return tb_in, tbs, tb_out


def _pad_lora(w, n_in):
    z = jnp.zeros_like(w[0])
    return [jnp.concatenate([w[0], z], axis=0).astype(BF16), jnp.concatenate([z, w[1]], axis=0).astype(BF16)]


def _trunk(x, mod3, prm, s5_tabs):
    bsz, t_len, _ = x.shape
    tb_in, tbs, tb_out = _block_sizes(t_len)
    ps, u = _in_call(x, mod3, prm["norm1_g"], prm["w_in"], prm["mu_shift"], tb_in)
    y_b = _scan_call(ps, prm["vecs"], prm["mats"], None, direction=1, tbs=tbs)
    y_rwkv = _scan_call(ps, prm["vecs"], prm["mats"], y_b, direction=0, tbs=tbs)
    z_s5 = _s5_call(u, s5_tabs)
    return _out_call(x, y_rwkv, z_s5, mod3, prm["s5_out_g"], prm["w_out"], prm["norm2_g"],
                     prm["w_ff1"], prm["w_ff3"], prm["w_ff2"], prm["final_g"], tb_out)


def _prepare(norm1_g, w_in, mu_shift, w0, w2, a0, a2, g2, k_k, k_a, r_k, lnx_g, lnx_b, s5_out_g, w_out,
             norm2_g, w_ff1, w_ff3, w_ff2, final_g):
    row = lambda v: v.reshape(1, -1)
    vecs = {
        "w0": [row(w0[0]), row(w0[1])], "a0": [row(a0[0]), row(a0[1])],
        "k_k": row(k_k), "k_a": row(k_a), "r_k": row(r_k), "lnx_g": row(lnx_g), "lnx_b": row(lnx_b),
    }
    mats = {"w2p": _pad_lora(w2, 64), "a2p": _pad_lora(a2, 64), "g2": g2.astype(BF16)}
    return {
        "norm1_g": row(norm1_g), "w_in": w_in.astype(BF16), "mu_shift": row(mu_shift),
        "vecs": vecs, "mats": mats, "s5_out_g": row(s5_out_g), "w_out": w_out.astype(BF16),
        "norm2_g": row(norm2_g), "w_ff1": w_ff1.astype(BF16), "w_ff3": w_ff3.astype(BF16),
        "w_ff2": w_ff2.astype(BF16), "final_g": row(final_g),
    }


def kernel(x_prompt, x_sample, c_prompt, c_sample, norm1_g, w_ada, b_ada, w_in, mu_shift, w0, w2, a0, a2, g2,
           k_k, k_a, r_k, lnx_g, lnx_b, lam_re, lam_im, log_dt, b_re, b_im, c_re, c_im, d_skip, w_glu, b_glu,
           s5_out_g, w_out, norm2_g, w_ff1, w_ff3, w_ff2, final_g):
    depth = norm1_g.shape[0]
    assert depth == 1, "the fused output kernel applies the final norm, so it supports a single layer"
    nbp = x_prompt.shape[0]
    xs = [x_prompt, x_sample]
    c_all = jnp.concatenate([c_prompt, c_sample], axis=0)
    max_t = max(x_prompt.shape[1], x_sample.shape[1])
    max_levels = max(1, (max_t // S5_CHUNK - 1).bit_length())
    for i in range(depth):
        prm = _prepare(norm1_g[i], w_in[i], mu_shift[i], w0[i], w2[i], a0[i], a2[i], g2[i], k_k[i], k_a[i],
                       r_k[i], lnx_g[i], lnx_b[i], s5_out_g[i], w_out[i], norm2_g[i], w_ff1[i], w_ff3[i],
                       w_ff2[i], final_g)
        s5_tabs = _s5_tables(lam_re[i], lam_im[i], log_dt[i], b_re[i], b_im[i], c_re[i], c_im[i],
                             d_skip[i], w_glu[i], b_glu[i], max_levels)
        mod = _mod_call(c_all, w_ada[i].astype(BF16), b_ada[i].reshape(1, -1))
        mod3 = mod.reshape(c_all.shape[0], 6, D_MODEL)
        xs = [_trunk(xs[0], mod3[:nbp], prm, s5_tabs), _trunk(xs[1], mod3[nbp:], prm, s5_tabs)]
    return tuple(xs)
```
